```python
import math
import jax, jax.numpy as jnp
from jax import lax
import numpy as np

D_MODEL = 1024
BATCH = 8
SEQ = 2048
DEPTH = 2
DEC_BATCH = 128
DEC_SEQ = 8
PAST_LEN = 16384
PAGE_SIZE = 128

F32 = jnp.float32
EPS = 1e-5
LB_FLOOR = 1e-20
BRANCH_W = D_MODEL // 2
N_BRANCH = 4
CONV_W = 4
CHUNK = 64
RG_BLOCKS = 8
RG_BLOCK = BRANCH_W // RG_BLOCKS
RG_C = 8.0
RET_HEADS = 4
RET_DK = BRANCH_W // RET_HEADS
RET_DV = BRANCH_W // RET_HEADS
ROPE_BASE = 10000.0
HG_HEADS = 4
HG_DK = BRANCH_W // HG_HEADS
HG_DV = BRANCH_W // HG_HEADS
SSM_HEADDIM = 64
SSM_HEADS = BRANCH_W // SSM_HEADDIM
SSM_GROUPS = 2
SSM_STATE = 128
SSM_XBC = BRANCH_W + 2 * SSM_GROUPS * SSM_STATE
PEER_HEADS = 8
N_KEYS = 128
N_EXPERTS = N_KEYS * N_KEYS
PEER_TOPK = 16
PEER_QDIM = 256
PEER_HALF = PEER_QDIM // 2
PEER_BLOCK = 128
ALPHA = (2.0 * DEPTH) ** 0.25
BETA = (8.0 * DEPTH) ** -0.25

SEG_SIZES = (BRANCH_W, BRANCH_W,
             RET_HEADS * RET_DK, RET_HEADS * RET_DK, RET_HEADS * RET_DV, RET_HEADS * RET_DV,
             HG_HEADS * HG_DK, HG_HEADS * HG_DK, HG_HEADS * HG_DV, HG_HEADS * HG_DV,
             BRANCH_W, SSM_XBC, SSM_HEADS,
             N_BRANCH * D_MODEL)
SPLIT_POINTS = tuple(sum(SEG_SIZES[:i + 1]) for i in range(len(SEG_SIZES) - 1))
N_IN = sum(SEG_SIZES)

kernel_name = 'hybrid_rglru_ret_hgrn2_ssd_peer_step'


def _ln(x):
    x = x.astype(F32)
    xc = x - jnp.mean(x, -1, keepdims=True)
    return xc * lax.rsqrt(jnp.mean(xc * xc, -1, keepdims=True) + EPS)


def _rms(x):
    x = x.astype(F32)
    return x * lax.rsqrt(jnp.mean(x * x, -1, keepdims=True) + EPS)


def causal_conv(u, buf, w, b):
    L = u.shape[1]
    ext = jnp.concatenate([buf.astype(u.dtype), u], axis=1)
    out = b + ext[:, 0:L] * w[0]
    for j in range(1, CONV_W):
        out = out + ext[:, j:j + L] * w[j]
    return out, ext[:, L:]


def _lin_combine(e1, e2):
    a1, b1 = e1
    a2, b2 = e2
    return a1 * a2, a2 * b1 + b2


def rotary(t, pos0):
    L, dk = t.shape[1], t.shape[-1]
    half = dk // 2
    pos = pos0 + jnp.arange(L, dtype=F32)
    inv = 1.0 / (ROPE_BASE ** jnp.linspace(0.0, 1.0, half, dtype=F32))
    ang = pos[:, None] * inv
    cos = jnp.cos(ang)[None, :, None, :]
    sin = jnp.sin(ang)[None, :, None, :]
    t1, t2 = t[..., :half], t[..., half:]
    return jnp.concatenate([t1 * cos - t2 * sin, t2 * cos + t1 * sin], axis=-1)


def _to_chunks(t, n, c):
    return jnp.moveaxis(t.reshape(t.shape[0], n, c, *t.shape[2:]), 1, 0)


def _from_chunks(t):
    t = jnp.moveaxis(t, 0, 1)
    return t.reshape(t.shape[0], t.shape[1] * t.shape[2], *t.shape[3:])


def _masked_decay(seg, mask):
    return jnp.where(mask, jnp.exp(jnp.where(mask, seg, 0.0)), 0.0)


def chunk_scalar_decay(q, k, v, log_a, s0):
    L = q.shape[1]
    c = math.gcd(L, CHUNK)
    n = L // c
    causal = jnp.tril(jnp.ones((c, c), bool))[None, :, :, None]

    def step(S, inp):
        qc, kc, vc, la = inp
        cum = jnp.cumsum(la, axis=1)
        seg = cum[:, :, None, :] - cum[:, None, :, :]
        dec = _masked_decay(seg, causal)
        att = jnp.einsum('bthd,bshd->btsh', qc, kc) * dec
        o = (jnp.einsum('btsh,bshv->bthv', att, vc)
             + jnp.einsum('bthd,bhdv->bthv', qc * jnp.exp(cum)[..., None], S))
        w_end = jnp.exp(cum[:, -1:, :] - cum)
        S = (jnp.exp(cum[:, -1, :])[:, :, None, None] * S
             + jnp.einsum('bshd,bshv->bhdv', kc * w_end[..., None], vc))
        return S, o

    S, o = lax.scan(step, s0.astype(F32),
                    (_to_chunks(q, n, c), _to_chunks(k, n, c), _to_chunks(v, n, c), _to_chunks(log_a, n, c)))
    return _from_chunks(o), S


def chunk_vector_decay(q, k, v, log_f, s0):
    L = q.shape[1]
    c = math.gcd(L, CHUNK)
    n = L // c
    causal = jnp.tril(jnp.ones((c, c), bool))[None, :, :, None, None]

    def step(S, inp):
        qc, kc, vc, lf = inp
        cum = jnp.cumsum(lf, axis=1)
        seg = cum[:, :, None] - cum[:, None, :]
        dec = _masked_decay(seg, causal)
        att = jnp.einsum('btshd,bshd->btsh', qc[:, :, None] * dec, kc)
        o = (jnp.einsum('btsh,bshv->bthv', att, vc)
             + jnp.einsum('bthd,bhdv->bthv', qc * jnp.exp(cum), S))
        w_end = jnp.exp(cum[:, -1:] - cum)
        S = (jnp.exp(cum[:, -1])[..., None] * S
             + jnp.einsum('bshd,bshv->bhdv', kc * w_end, vc))
        return S, o

    S, o = lax.scan(step, s0.astype(F32),
                    (_to_chunks(q, n, c), _to_chunks(k, n, c), _to_chunks(v, n, c), _to_chunks(log_f, n, c)))
    return _from_chunks(o), S


def token_mixers(x, pos0, lp, lower, state):
    rg_h, rg_buf, ret_s, hg_s, ssm_s, ssm_buf = state
    Bn, L, _ = x.shape
    proj = jnp.einsum('bld,dn->bln', x, lp['w_in']).astype(F32)
    (rg_x, rg_z, ret_q, ret_k, ret_v, ret_g, hg_q, hg_f, hg_i, hg_g,
     ssm_z, ssm_xbc, ssm_dt, merge) = jnp.split(proj, SPLIT_POINTS, axis=-1)

    u, rg_buf_new = causal_conv(rg_x, rg_buf, lp['rg_conv_w'], lp['rg_conv_b'])
    ub = u.reshape(Bn, L, RG_BLOCKS, RG_BLOCK)
    r = jax.nn.sigmoid(jnp.einsum('blki,kij->blkj', ub, lp['rg_wa']).reshape(Bn, L, BRANCH_W) + lp['rg_ba'])
    ig = jax.nn.sigmoid(jnp.einsum('blki,kij->blkj', ub, lp['rg_wx']).reshape(Bn, L, BRANCH_W) + lp['rg_bx'])
    log_a = -RG_C * r * jax.nn.softplus(-lp['rg_lambda'].astype(F32))
    a = jnp.exp(log_a)
    b = jnp.sqrt(-jnp.expm1(2.0 * log_a)) * (ig * u)
    a_cum, h = lax.associative_scan(_lin_combine, (a, b), axis=1)
    h = h + a_cum * rg_h.astype(F32)[:, None, :]
    rg_out = h * jax.nn.gelu(rg_z, approximate=False)

    q = rotary(ret_q.reshape(Bn, L, RET_HEADS, RET_DK), pos0)
    k = rotary(ret_k.reshape(Bn, L, RET_HEADS, RET_DK), pos0) * (RET_DK ** -0.5)
    v = ret_v.reshape(Bn, L, RET_HEADS, RET_DV)
    log_gamma = jnp.log1p(-jnp.exp2(-5.0 - jnp.arange(RET_HEADS, dtype=F32)))
    o, ret_s_new = chunk_scalar_decay(q, k, v, jnp.broadcast_to(log_gamma, (Bn, L, RET_HEADS)), ret_s)
    o = _ln(o) * lp['ret_norm_g'].reshape(RET_HEADS, RET_DV)
    ret_out = jax.nn.silu(ret_g) * o.reshape(Bn, L, BRANCH_W)

    log_f = jnp.logaddexp(jnp.log(jnp.maximum(lower, LB_FLOOR)), jnp.log1p(-lower) + jax.nn.log_sigmoid(hg_f))
    log_f = log_f.reshape(Bn, L, HG_HEADS, HG_DK)
    o, hg_s_new = chunk_vector_decay(hg_q.reshape(Bn, L, HG_HEADS, HG_DK), -jnp.expm1(log_f),
                                     hg_i.reshape(Bn, L, HG_HEADS, HG_DV), log_f, hg_s)
    o = _rms(o) * lp['hg_norm_g'].reshape(HG_HEADS, HG_DV)
    hg_out = jax.nn.silu(hg_g) * o.reshape(Bn, L, BRANCH_W)

    xbc, ssm_buf_new = causal_conv(ssm_xbc, ssm_buf, lp['ssm_conv_w'], lp['ssm_conv_b'])
    xbc = jax.nn.silu(xbc)
    xs, bm, cm = jnp.split(xbc, (BRANCH_W, BRANCH_W + SSM_GROUPS * SSM_STATE), axis=-1)
    xs = xs.reshape(Bn, L, SSM_HEADS, SSM_HEADDIM)
    rep = SSM_HEADS // SSM_GROUPS
    bm = jnp.repeat(bm.reshape(Bn, L, SSM_GROUPS, SSM_STATE), rep, axis=2)
    cm = jnp.repeat(cm.reshape(Bn, L, SSM_GROUPS, SSM_STATE), rep, axis=2)
    dt = jax.nn.softplus(ssm_dt + lp['ssm_dt_bias'])
    la = -dt * jnp.exp(lp['ssm_a_log'].astype(F32))
    y, ssm_s_new = chunk_scalar_decay(cm, bm * dt[..., None], xs, la, ssm_s)
    y = y + lp['ssm_d'][:, None] * xs
    y = (y.reshape(Bn, L, BRANCH_W) * jax.nn.silu(ssm_z)).reshape(Bn, L, SSM_GROUPS, BRANCH_W // SSM_GROUPS)
    ssm_out = _rms(y).reshape(Bn, L, BRANCH_W) * lp['ssm_norm_g']

    br = jnp.stack([rg_out, ret_out, hg_out, ssm_out], axis=2)
    pb = jnp.einsum('blkc,kcd->blkd', br, lp['w_branch'])
    gates = jax.nn.sigmoid(merge.reshape(Bn, L, N_BRANCH, D_MODEL))
    mix = jnp.einsum('bld,de->ble', jnp.sum(gates * pb, axis=2), lp['w_out'].astype(F32))
    new_state = (h[:, -1].astype(rg_h.dtype), rg_buf_new.astype(rg_buf.dtype),
                 ret_s_new.astype(ret_s.dtype), hg_s_new.astype(hg_s.dtype),
                 ssm_s_new.astype(ssm_s.dtype), ssm_buf_new.astype(ssm_buf.dtype))
    return mix, new_state


def peer(x, wq, sub_keys, u_tab, v_tab):
    T, D = x.shape
    blk = math.gcd(T, PEER_BLOCK)
    kk = PEER_TOPK * PEER_TOPK

    def one_block(xb):
        q = jnp.einsum('td,dn->tn', xb, wq).astype(F32).reshape(blk, PEER_HEADS, 2, PEER_HALF)
        s = jnp.einsum('thpc,pnc->thpn', q, sub_keys.astype(F32))
        s1, i1 = lax.top_k(s[:, :, 0], PEER_TOPK)
        s2, i2 = lax.top_k(s[:, :, 1], PEER_TOPK)
        cand = (s1[..., :, None] + s2[..., None, :]).reshape(blk, PEER_HEADS, kk)
        cidx = (i1[..., :, None] * N_KEYS + i2[..., None, :]).reshape(blk, PEER_HEADS, kk)
        sc, j = lax.top_k(cand, PEER_TOPK)
        idx = jnp.take_along_axis(cidx, j, axis=-1)
        g = jax.nn.softmax(sc, axis=-1)
        ue = jnp.take(u_tab, idx, axis=0)
        ve = jnp.take(v_tab, idx, axis=0)
        hdn = jax.nn.gelu(jnp.einsum('thkd,td->thk', ue, xb).astype(F32), approximate=False)
        return jnp.einsum('thk,thkd->td', g * hdn, ve.astype(F32))

    out = lax.map(one_block, x.reshape(T // blk, blk, D))
    return out.reshape(T, D)


def layer(x, pos0, lp, lower, state):
    Bn, L, D = x.shape
    mix, new_state = token_mixers(x, pos0, lp, lower, state)
    x = (_ln(ALPHA * x.astype(F32) + mix) * lp['ln1_g'] + lp['ln1_b']).astype(x.dtype)
    ff = peer(x.reshape(Bn * L, D), lp['peer_wq'], lp['peer_keys'], lp['peer_u'], lp['peer_v']).reshape(Bn, L, D)
    x = (_ln(ALPHA * x.astype(F32) + ff) * lp['ln2_g'] + lp['ln2_b']).astype(x.dtype)
    return x, new_state


def zero_state(n, dtype):
    return (jnp.zeros((n, BRANCH_W), dtype),
            jnp.zeros((n, CONV_W - 1, BRANCH_W), dtype),
            jnp.zeros((n, RET_HEADS, RET_DK, RET_DV), dtype),
            jnp.zeros((n, HG_HEADS, HG_DK, HG_DV), dtype),
            jnp.zeros((n, SSM_HEADS, SSM_STATE, SSM_HEADDIM), dtype),
            jnp.zeros((n, CONV_W - 1, SSM_XBC), dtype))


def setup_inputs(seed: int = 0) -> dict:
    key = jax.random.key(seed)
    keys = jax.random.split(key, 64)
    cnt = [0]

    def nxt():
        k = keys[cnt[0]]
        cnt[0] += 1
        return k

    def nrm(shape, scale):
        return jax.random.normal(nxt(), shape, F32) * scale

    def unif(shape, lo, hi):
        return jax.random.uniform(nxt(), shape, F32, lo, hi)

    a0 = unif((DEPTH, BRANCH_W), 0.9, 0.999) ** (1.0 / RG_C)
    dt0 = jnp.exp(unif((DEPTH, SSM_HEADS), math.log(1e-3), math.log(1e-1)))
    return {
        'x_prompt': nrm((BATCH, SEQ, D_MODEL), 1.0),
        'x_sample': nrm((DEC_BATCH, DEC_SEQ, D_MODEL), 1.0),
        'state_rglru_h': nrm((DEPTH, DEC_BATCH, BRANCH_W), 0.5),
        'state_rglru_conv': nrm((DEPTH, DEC_BATCH, CONV_W - 1, BRANCH_W), 1.0),
        'state_ret': nrm((DEPTH, DEC_BATCH, RET_HEADS, RET_DK, RET_DV), 0.5),
        'state_hgrn': nrm((DEPTH, DEC_BATCH, HG_HEADS, HG_DK, HG_DV), 0.5),
        'state_ssm': nrm((DEPTH, DEC_BATCH, SSM_HEADS, SSM_STATE, SSM_HEADDIM), 0.5),
        'state_ssm_conv': nrm((DEPTH, DEC_BATCH, CONV_W - 1, SSM_XBC), 1.0),
        'w_in': nrm((DEPTH, D_MODEL, N_IN), D_MODEL ** -0.5),
        'rg_conv_w': nrm((DEPTH, CONV_W, BRANCH_W), CONV_W ** -0.5),
        'rg_conv_b': nrm((DEPTH, BRANCH_W), 0.01),
        'rg_wa': nrm((DEPTH, RG_BLOCKS, RG_BLOCK, RG_BLOCK), RG_BLOCK ** -0.5),
        'rg_ba': nrm((DEPTH, BRANCH_W), 0.01),
        'rg_wx': nrm((DEPTH, RG_BLOCKS, RG_BLOCK, RG_BLOCK), RG_BLOCK ** -0.5),
        'rg_bx': nrm((DEPTH, BRANCH_W), 0.01),
        'rg_lambda': jnp.log(a0) - jnp.log1p(-a0),
        'ret_norm_g': 1.0 + nrm((DEPTH, BRANCH_W), 0.02),
        'hg_lower': nrm((DEPTH, HG_HEADS * HG_DK), 1.0),
        'hg_norm_g': 1.0 + nrm((DEPTH, BRANCH_W), 0.02),
        'ssm_conv_w': nrm((DEPTH, CONV_W, SSM_XBC), CONV_W ** -0.5),
        'ssm_conv_b': nrm((DEPTH, SSM_XBC), 0.01),
        'ssm_dt_bias': dt0 + jnp.log(-jnp.expm1(-dt0)),
        'ssm_a_log': jnp.log(unif((DEPTH, SSM_HEADS), 1.0, 16.0)),
        'ssm_d': 1.0 + nrm((DEPTH, SSM_HEADS), 0.1),
        'ssm_norm_g': 1.0 + nrm((DEPTH, BRANCH_W), 0.02),
        'w_branch': nrm((DEPTH, N_BRANCH, BRANCH_W, D_MODEL), BETA * BRANCH_W ** -0.5),
        'w_out': nrm((DEPTH, D_MODEL, D_MODEL), BETA * D_MODEL ** -0.5),
        'ln1_g': 1.0 + nrm((DEPTH, D_MODEL), 0.02),
        'ln1_b': nrm((DEPTH, D_MODEL), 0.01),
        'peer_wq': nrm((DEPTH, D_MODEL, PEER_HEADS * PEER_QDIM), D_MODEL ** -0.5),
        'peer_keys': nrm((DEPTH, 2, N_KEYS, PEER_HALF), PEER_HALF ** -0.5),
        'peer_u': nrm((DEPTH, N_EXPERTS, D_MODEL), D_MODEL ** -0.5),
        'peer_v': nrm((DEPTH, N_EXPERTS, D_MODEL), BETA),
        'ln2_g': 1.0 + nrm((DEPTH, D_MODEL), 0.02),
        'ln2_b': nrm((DEPTH, D_MODEL), 0.01),
    }


def reference(x_prompt, x_sample, state_rglru_h, state_rglru_conv, state_ret, state_hgrn, state_ssm,
              state_ssm_conv, w_in, rg_conv_w, rg_conv_b, rg_wa, rg_ba, rg_wx, rg_bx, rg_lambda, ret_norm_g,
              hg_lower, hg_norm_g, ssm_conv_w, ssm_conv_b, ssm_dt_bias, ssm_a_log, ssm_d, ssm_norm_g,
              w_branch, w_out, ln1_g, ln1_b, peer_wq, peer_keys, peer_u, peer_v, ln2_g, ln2_b):
    p_lb = jax.nn.softmax(hg_lower.astype(F32), axis=0)
    lower = jnp.cumsum(p_lb, axis=0) - p_lb[0]
    yp, ys = x_prompt, x_sample
    new_p, new_s = [], []
    for l in range(DEPTH):
        lp = {'w_in': w_in[l], 'rg_conv_w': rg_conv_w[l], 'rg_conv_b': rg_conv_b[l], 'rg_wa': rg_wa[l],
              'rg_ba': rg_ba[l], 'rg_wx': rg_wx[l], 'rg_bx': rg_bx[l], 'rg_lambda': rg_lambda[l],
              'ret_norm_g': ret_norm_g[l], 'hg_norm_g': hg_norm_g[l], 'ssm_conv_w': ssm_conv_w[l],
              'ssm_conv_b': ssm_conv_b[l], 'ssm_dt_bias': ssm_dt_bias[l], 'ssm_a_log': ssm_a_log[l],
              'ssm_d': ssm_d[l], 'ssm_norm_g': ssm_norm_g[l], 'w_branch': w_branch[l], 'w_out': w_out[l],
              'ln1_g': ln1_g[l], 'ln1_b': ln1_b[l], 'peer_wq': peer_wq[l], 'peer_keys': peer_keys[l],
              'peer_u': peer_u[l], 'peer_v': peer_v[l], 'ln2_g': ln2_g[l], 'ln2_b': ln2_b[l]}
        yp, sp = layer(yp, 0, lp, lower[l], zero_state(yp.shape[0], state_ret.dtype))
        ys, ss = layer(ys, PAST_LEN, lp, lower[l],
                       (state_rglru_h[l], state_rglru_conv[l], state_ret[l], state_hgrn[l],
                        state_ssm[l], state_ssm_conv[l]))
        new_p.append(sp)
        new_s.append(ss)
    p_rglru_h, p_rglru_conv, p_ret, p_hgrn, p_ssm, p_ssm_conv = [jnp.stack(t) for t in zip(*new_p)]
    s_rglru_h, s_rglru_conv, s_ret, s_hgrn, s_ssm, s_ssm_conv = [jnp.stack(t) for t in zip(*new_s)]
    return (yp, ys, p_rglru_h, p_rglru_conv, p_ret, p_hgrn, p_ssm, p_ssm_conv,
            s_rglru_h, s_rglru_conv, s_ret, s_hgrn, s_ssm, s_ssm_conv)
```

```python
import functools
import math

import numpy as np
import jax
import jax.numpy as jnp
from jax import lax
from jax.experimental import pallas as pl
from jax.experimental.pallas import tpu as pltpu

F32 = jnp.float32
BF16 = jnp.bfloat16
I32 = jnp.int32

D_MODEL = 1024
DEPTH = 2
PAST_LEN = 16384
EPS = 1e-5
LB_FLOOR = 1e-20
BRANCH_W = D_MODEL // 2
N_BRANCH = 4
CONV_W = 4
RG_BLOCKS = 8
RG_BLOCK = BRANCH_W // RG_BLOCKS
RG_C = 8.0
RET_HEADS = 4
RET_DK = BRANCH_W // RET_HEADS
ROPE_BASE = 10000.0
HG_HEADS = 4
HG_DK = BRANCH_W // HG_HEADS
SSM_HEADDIM = 64
SSM_HEADS = BRANCH_W // SSM_HEADDIM
SSM_GROUPS = 2
SSM_STATE = 128
SSM_XBC = BRANCH_W + 2 * SSM_GROUPS * SSM_STATE
PEER_HEADS = 8
N_KEYS = 128
N_EXPERTS = N_KEYS * N_KEYS
PEER_TOPK = 16
PEER_QDIM = 256
PEER_HALF = PEER_QDIM // 2
ALPHA = (2.0 * DEPTH) ** 0.25

V7X_LANES = 128
V7X_SUBLANES = 8
V7X_VMEM_LIMIT_BYTES = 56 * 1024 * 1024

COL_RG = 0
COL_RET = 1024
COL_HG = 3072
COL_XBC = 5120
COL_GATE = 6144
COL_Z = 10240
COL_DT = 10752
N_PROJ = 10880
DT_PAD = 128

HG_CHUNK = 64
HG_SUB = 16
NEG_INF = float("-inf")


def _cparams(sem):
    return pltpu.CompilerParams(dimension_semantics=sem, vmem_limit_bytes=V7X_VMEM_LIMIT_BYTES)


def _full_spec(arr):
    zeros = (0,) * arr.ndim
    return pl.BlockSpec(arr.shape, lambda *_: zeros)


def _sigmoid(x):
    return 1.0 / (1.0 + jnp.exp(-x))


def _silu(x):
    return x * _sigmoid(x)


def _softplus(x):
    return jnp.maximum(x, 0.0) + jnp.log(1.0 + jnp.exp(-jnp.abs(x)))


def _gelu(x):
    return 0.5 * x * (1.0 + lax.erf(x * (1.0 / math.sqrt(2.0))))


def _one_minus_exp(x):
    e = jnp.exp(x)
    le = jnp.log(e)
    safe = jnp.where(le == 0.0, 1.0, le)
    return jnp.where(le == 0.0, -x, (1.0 - e) * x / safe)


def _row_iota(shape):
    return lax.broadcasted_iota(I32, shape, 0)


def _shift_rows(x, d, fill):
    rolled = pltpu.roll(x, d, axis=0)
    return jnp.where(_row_iota(x.shape) >= d, rolled, fill)


def _cumsum_rows(x):
    n = x.shape[0]
    d = 1
    while d < n:
        x = x + _shift_rows(x, d, 0.0)
        d *= 2
    return x


def _dot(a, b):
    return jnp.dot(a, b, preferred_element_type=F32)


def _dot_nt(a, b):
    return lax.dot_general(a, b, (((1,), (1,)), ((), ())), preferred_element_type=F32)


def _dot_tn(a, b):
    return lax.dot_general(a, b, (((0,), (0,)), ((), ())), preferred_element_type=F32)


def _layer_norm_rows(y, g, b):
    mu = jnp.mean(y, axis=-1, keepdims=True)
    yc = y - mu
    var = jnp.mean(yc * yc, axis=-1, keepdims=True)
    return yc * lax.rsqrt(var + EPS) * g + b


def _matmul_kernel(x_ref, w_ref, o_ref):
    o_ref[...] = _dot(x_ref[...], w_ref[...])


def _in_proj(xb, w):
    t = xb.shape[0]
    tm = min(512, t)
    tn = N_PROJ // 5
    return pl.pallas_call(
        _matmul_kernel,
        out_shape=jax.ShapeDtypeStruct((t, N_PROJ), F32),
        grid=(N_PROJ // tn, t // tm),
        in_specs=[pl.BlockSpec((tm, D_MODEL), lambda j, i: (i, 0)),
                  pl.BlockSpec((D_MODEL, tn), lambda j, i: (0, j))],
        out_specs=pl.BlockSpec((tm, tn), lambda j, i: (i, j)),
        compiler_params=_cparams(("arbitrary", "arbitrary")),
        name="in_proj",
    )(xb, w)


def _conv_taps(x, prev8, w_ref, b_ref):
    tb = x.shape[0]
    row8 = _row_iota((V7X_SUBLANES, x.shape[1]))
    out = b_ref[...] + x * w_ref[CONV_W - 1:CONV_W, :]
    for k in range(1, CONV_W):
        r = pltpu.roll(x, k, axis=0)
        p = pltpu.roll(prev8, k, axis=0)
        top = jnp.where(row8 >= k, r[0:V7X_SUBLANES], p)
        sh = top if tb == V7X_SUBLANES else jnp.concatenate([top, r[V7X_SUBLANES:]], axis=0)
        out = out + sh * w_ref[CONV_W - 1 - k:CONV_W - k, :]
    return out


def _rg_kernel(xz_ref, prev_ref, h0_ref, cw_ref, cb_ref, wax_ref, bax_ref, lam_ref,
               out_ref, tail_ref, hl_ref, *, sb, tb):
    i = pl.program_id(1)

    @pl.when(i == 0)
    def _():
        tail_ref[...] = prev_ref[...]
        hl_ref[...] = jnp.broadcast_to(h0_ref[...], hl_ref.shape)

    sp8 = -RG_C * _softplus(-lam_ref[...])
    row = _row_iota((tb, BRANCH_W))
    for s in range(sb):
        rows = slice(s * tb, (s + 1) * tb)
        x = xz_ref[rows, 0:BRANCH_W]
        z = xz_ref[rows, BRANCH_W:2 * BRANCH_W]
        u = _conv_taps(x, tail_ref[s], cw_ref, cb_ref)
        gates = _dot(u.astype(BF16), wax_ref[...]) + bax_ref[...]
        r = _sigmoid(gates[:, 0:BRANCH_W])
        ig = _sigmoid(gates[:, BRANCH_W:2 * BRANCH_W])
        log_a = sp8 * r
        a = jnp.exp(log_a)
        b = jnp.sqrt(_one_minus_exp(2.0 * log_a)) * (ig * u)
        d = 1
        while d < tb:
            a_s = jnp.where(row >= d, pltpu.roll(a, d, axis=0), 1.0)
            b_s = jnp.where(row >= d, pltpu.roll(b, d, axis=0), 0.0)
            b = a * b_s + b
            a = a * a_s
            d *= 2
        h = b + a * hl_ref[s, V7X_SUBLANES - 1:V7X_SUBLANES, :]
        out_ref[rows, :] = (h * _gelu(z)).astype(out_ref.dtype)
        tail_ref[s] = x[tb - V7X_SUBLANES:tb]
        hl_ref[s] = h[tb - V7X_SUBLANES:tb]


def _rg_mixer(proj, prev8, h0, cw, cb, wax, bax, lam, *, bn, seq, sb, tb):
    nt = seq // tb
    rows = sb * tb
    kern = functools.partial(_rg_kernel, sb=sb, tb=tb)
    st_spec = pl.BlockSpec((sb, V7X_SUBLANES, BRANCH_W), lambda b, i: (b, 0, 0))
    return pl.pallas_call(
        kern,
        out_shape=(jax.ShapeDtypeStruct((bn * seq, BRANCH_W), BF16),
                   jax.ShapeDtypeStruct((bn, V7X_SUBLANES, BRANCH_W), F32),
                   jax.ShapeDtypeStruct((bn, V7X_SUBLANES, BRANCH_W), F32)),
        grid=(bn // sb, nt),
        in_specs=[pl.BlockSpec((rows, 2 * BRANCH_W), lambda b, i: (b * nt + i, COL_RG // (2 * BRANCH_W))),
                  st_spec,
                  pl.BlockSpec((sb, 1, BRANCH_W), lambda b, i: (b, 0, 0)),
                  _full_spec(cw), _full_spec(cb), _full_spec(wax), _full_spec(bax), _full_spec(lam)],
        out_specs=(pl.BlockSpec((rows, BRANCH_W), lambda b, i: (b * nt + i, 0)), st_spec, st_spec),
        compiler_params=_cparams(("arbitrary", "arbitrary")),
        name="rg_mixer",
    )(proj, prev8, h0, cw, cb, wax, bax, lam)


_RET_LOG_GAMMA = tuple(math.log1p(-2.0 ** (-5.0 - h)) for h in range(RET_HEADS))


def _ret_kernel(qk_ref, vg_ref, cs_ref, s0_ref, ng_ref, out_ref, s_ref, dmat, gq, we, *, sb, tb):
    b = pl.program_id(0)
    i = pl.program_id(1)

    @pl.when((b == 0) & (i == 0))
    def _():
        dif = (lax.broadcasted_iota(I32, (tb, tb), 0) - lax.broadcasted_iota(I32, (tb, tb), 1)).astype(F32)
        rowf = lax.broadcasted_iota(I32, (tb, RET_DK), 0).astype(F32)
        for h in range(RET_HEADS):
            lg = _RET_LOG_GAMMA[h]
            dmat[h] = jnp.where(dif >= 0.0, jnp.exp(jnp.maximum(dif, 0.0) * lg), 0.0)
            gq[h] = jnp.exp((rowf + 1.0) * lg)
            we[h] = jnp.exp((tb - 1.0 - rowf) * lg)

    @pl.when(i == 0)
    def _():
        s_ref[...] = s0_ref[...]

    cos2 = cs_ref[0]
    sin2 = cs_ref[1]
    for s in range(sb):
        rows = slice(s * tb, (s + 1) * tb)
        outs = []
        for h in range(RET_HEADS):
            c0 = h * RET_DK
            qh = qk_ref[rows, c0:c0 + RET_DK]
            kh = qk_ref[rows, BRANCH_W + c0:BRANCH_W + c0 + RET_DK]
            vh = vg_ref[rows, c0:c0 + RET_DK]
            gh = vg_ref[rows, BRANCH_W + c0:BRANCH_W + c0 + RET_DK]
            qr = qh * cos2 + pltpu.roll(qh, RET_DK // 2, axis=1) * sin2
            kr = (kh * cos2 + pltpu.roll(kh, RET_DK // 2, axis=1) * sin2) * (RET_DK ** -0.5)
            st = s_ref[s, h]
            att = _dot_nt(qr, kr) * dmat[h]
            o = _dot(att, vh) + _dot(qr * gq[h], st)
            s_ref[s, h] = math.exp(tb * _RET_LOG_GAMMA[h]) * st + _dot_tn(kr * we[h], vh)
            mu = jnp.mean(o, axis=-1, keepdims=True)
            oc = o - mu
            var = jnp.mean(oc * oc, axis=-1, keepdims=True)
            on = oc * lax.rsqrt(var + EPS) * ng_ref[:, c0:c0 + RET_DK]
            outs.append(_silu(gh) * on)
        out_ref[rows, :] = jnp.concatenate(outs, axis=1).astype(out_ref.dtype)


def _ret_mixer(proj, cs_tab, s0, ng, *, bn, seq, sb, tb):
    nt = seq // tb
    rows = sb * tb
    kern = functools.partial(_ret_kernel, sb=sb, tb=tb)
    st_spec = pl.BlockSpec((sb, RET_HEADS, RET_DK, RET_DK), lambda b, i: (b, 0, 0, 0))
    cb = COL_RET // (2 * BRANCH_W)
    return pl.pallas_call(
        kern,
        out_shape=(jax.ShapeDtypeStruct((bn * seq, BRANCH_W), BF16),
                   jax.ShapeDtypeStruct((bn, RET_HEADS, RET_DK, RET_DK), F32)),
        grid=(bn // sb, nt),
        in_specs=[pl.BlockSpec((rows, 2 * BRANCH_W), lambda b, i: (b * nt + i, cb)),
                  pl.BlockSpec((rows, 2 * BRANCH_W), lambda b, i: (b * nt + i, cb + 1)),
                  pl.BlockSpec((2, tb, RET_DK), lambda b, i: (0, i, 0)),
                  st_spec, _full_spec(ng)],
        out_specs=(pl.BlockSpec((rows, BRANCH_W), lambda b, i: (b * nt + i, 0)), st_spec),
        scratch_shapes=[pltpu.VMEM((RET_HEADS, tb, tb), F32),
                        pltpu.VMEM((RET_HEADS, tb, RET_DK), F32),
                        pltpu.VMEM((RET_HEADS, tb, RET_DK), F32)],
        compiler_params=_cparams(("arbitrary", "arbitrary")),
        name="ret_mixer",
    )(proj, proj, cs_tab, s0, ng)


def _rotary_tables(pos0, seq):
    half = RET_DK // 2
    pos = pos0 + jnp.arange(seq, dtype=F32)
    inv = 1.0 / (ROPE_BASE ** jnp.linspace(0.0, 1.0, half, dtype=F32))
    ang = pos[:, None] * inv
    cos, sin = jnp.cos(ang), jnp.sin(ang)
    return jnp.stack([jnp.concatenate([cos, cos], axis=1), jnp.concatenate([-sin, sin], axis=1)])


def _block_ones():
    blk = np.kron(np.eye(HG_HEADS, dtype=np.float32), np.ones((HG_DK, HG_DK), np.float32))
    return jnp.asarray(blk, BF16)


def _hg_lower(hgl_ref, layer):
    raw = hgl_ref[...]
    e = jnp.exp(raw - jnp.max(raw, axis=0, keepdims=True))
    p = e / jnp.sum(e, axis=0, keepdims=True)
    cum = p[0:1]
    for l in range(1, layer + 1):
        cum = cum + p[l:l + 1]
    return cum - p[0:1]


def _hg_chunk(q, f, vi, st_refs, log_lb, log_1mlb, ones_ref, c, w):
    ls = jnp.minimum(f, 0.0) - jnp.log(1.0 + jnp.exp(-jnp.abs(f)))
    b = log_1mlb + ls
    lf = jnp.maximum(log_lb, b) + jnp.log(1.0 + jnp.exp(-jnp.abs(log_lb - b)))
    kk = _one_minus_exp(lf)
    cum = _cumsum_rows(lf) if c > 1 else lf
    rmod = _row_iota((c, BRANCH_W)) % w
    p0 = q * kk
    o = _dot(p0.astype(BF16), ones_ref[...]) * vi
    for delta in range(1, w):
        dec = jnp.exp(jnp.minimum(cum - pltpu.roll(cum, delta, axis=0), 0.0))
        p = q * dec * pltpu.roll(kk, delta, axis=0)
        rs = _dot(p.astype(BF16), ones_ref[...])
        o = o + jnp.where(rmod >= delta, rs, 0.0) * pltpu.roll(vi, delta, axis=0)
    nsub = c // w
    if nsub > 1:
        off = [jnp.zeros((w, BRANCH_W), F32)]
        for i in range(1, nsub):
            ref_row = cum[i * w - 1:i * w, :]
            qt = q[i * w:(i + 1) * w] * jnp.exp(cum[i * w:(i + 1) * w] - ref_row)
            kt = kk[0:i * w] * jnp.exp(ref_row - cum[0:i * w])
            parts = []
            for h in range(HG_HEADS):
                hs = slice(h * HG_DK, (h + 1) * HG_DK)
                att = _dot_nt(qt[:, hs], kt[:, hs])
                parts.append(_dot(att, vi[0:i * w, hs]))
            off.append(jnp.concatenate(parts, axis=1))
        o = o + jnp.concatenate(off, axis=0)
    qg = q * jnp.exp(cum)
    cl = cum[c - 1:c, :]
    kw = kk * jnp.exp(cl - cum)
    dec_end = jnp.exp(cl)
    inter = []
    for h in range(HG_HEADS):
        hs = slice(h * HG_DK, (h + 1) * HG_DK)
        st = st_refs(h)
        inter.append(_dot_nt(qg[:, hs], st[...]))
        st[...] = st[...] * dec_end[:, hs] + _dot_tn(vi[:, hs], kw[:, hs])
    return o + jnp.concatenate(inter, axis=1)


def _hg_kernel(qf_ref, ig_ref, hgl_ref, ones_ref, s0_ref, ng_ref, out_ref, s_ref, *, sb, tb, c, w, layer, nt):
    i = pl.program_id(1)

    @pl.when(i == 0)
    def _():
        for s in range(sb):
            for h in range(HG_HEADS):
                s_ref[s, h] = s0_ref[s, h].T

    lower = _hg_lower(hgl_ref, layer)
    log_lb = jnp.log(jnp.maximum(lower, LB_FLOOR))
    log_1mlb = jnp.log(1.0 - lower)
    for s in range(sb):
        def chunk(ci, carry):
            r0 = pl.multiple_of(s * tb + ci * c, V7X_SUBLANES)
            rows = pl.ds(r0, c)
            q = qf_ref[rows, 0:BRANCH_W]
            f = qf_ref[rows, BRANCH_W:2 * BRANCH_W]
            vi = ig_ref[rows, 0:BRANCH_W]
            g = ig_ref[rows, BRANCH_W:2 * BRANCH_W]
            o = _hg_chunk(q, f, vi, lambda h: s_ref.at[s, h], log_lb, log_1mlb, ones_ref, c, w)
            outs = []
            for h in range(HG_HEADS):
                hs = slice(h * HG_DK, (h + 1) * HG_DK)
                oh = o[:, hs]
                ms = jnp.mean(oh * oh, axis=-1, keepdims=True)
                outs.append(oh * lax.rsqrt(ms + EPS))
            on = jnp.concatenate(outs, axis=1) * ng_ref[...]
            out_ref[rows, :] = (_silu(g) * on).astype(out_ref.dtype)
            return carry

        if tb == c:
            chunk(0, 0)
        else:
            lax.fori_loop(0, tb // c, chunk, 0)

    @pl.when(i == nt - 1)
    def _():
        for s in range(sb):
            for h in range(HG_HEADS):
                s_ref[s, h] = s_ref[s, h].T


def _hg_mixer(proj, hgl, ones_bd, s0, ng, *, bn, seq, sb, tb, layer):
    nt = seq // tb
    rows = sb * tb
    c = min(HG_CHUNK, tb)
    w = min(HG_SUB, c)
    kern = functools.partial(_hg_kernel, sb=sb, tb=tb, c=c, w=w, layer=layer, nt=nt)
    st_spec = pl.BlockSpec((sb, HG_HEADS, HG_DK, HG_DK), lambda b, i: (b, 0, 0, 0))
    cb = COL_HG // (2 * BRANCH_W)
    return pl.pallas_call(
        kern,
        out_shape=(jax.ShapeDtypeStruct((bn * seq, BRANCH_W), BF16),
                   jax.ShapeDtypeStruct((bn, HG_HEADS, HG_DK, HG_DK), F32)),
        grid=(bn // sb, nt),
        in_specs=[pl.BlockSpec((rows, 2 * BRANCH_W), lambda b, i: (b * nt + i, cb)),
                  pl.BlockSpec((rows, 2 * BRANCH_W), lambda b, i: (b * nt + i, cb + 1)),
                  _full_spec(hgl), _full_spec(ones_bd), st_spec, _full_spec(ng)],
        out_specs=(pl.BlockSpec((rows, BRANCH_W), lambda b, i: (b * nt + i, 0)), st_spec),
        compiler_params=_cparams(("arbitrary", "arbitrary")),
        name="hg_mixer",
    )(proj, proj, hgl, ones_bd, s0, ng)


SSM_GW = SSM_GROUPS * SSM_STATE
SSM_HPG = SSM_HEADS // SSM_GROUPS


def _ssd_prep(proj, cw, cb, dt_bias, a_log, dvec, ng):
    dt_t = proj[:, COL_DT:COL_DT + SSM_HEADS].T
    pad = DT_PAD - SSM_HEADS
    dtb = jnp.pad(dt_bias, (0, pad))[None, :]
    alog = jnp.pad(a_log, (0, pad))[None, :]
    return (dt_t, cw, cb[None, :], dtb, alog, dt_bias[:, None], a_log[:, None],
            jnp.repeat(dvec, SSM_HEADDIM)[None, :], ng[None, :])


def _seg_cumsum(x, c, axis):
    pos = lax.broadcasted_iota(I32, x.shape, axis) % c
    d = 1
    while d < c:
        x = x + jnp.where(pos >= d, pltpu.roll(x, d, axis=axis), 0.0)
        d *= 2
    return x


def _ssd_kernel(xbc_ref, z_ref, dt_ref, dtt_ref, prev_ref, s0_ref, cw_ref, cb_ref, dtb_ref, alog_ref, dtbc_ref,
                alogc_ref, dvec_ref, ng_ref, out_ref, tail_ref, s_ref, *, sb, tb, c):
    i = pl.program_id(1)

    @pl.when(i == 0)
    def _():
        tail_ref[...] = prev_ref[...]
        s_ref[...] = s0_ref[...]

    dt_all = _softplus(dt_ref[...] + dtb_ref[...])
    cum_all = _seg_cumsum(-dt_all * jnp.exp(alog_ref[...]), c, 0)
    dtt_all = _softplus(dtt_ref[...] + dtbc_ref[...])
    cumt_all = _seg_cumsum(-dtt_all * jnp.exp(alogc_ref[...]), c, 1)
    causal = lax.broadcasted_iota(I32, (c, c), 0) >= lax.broadcasted_iota(I32, (c, c), 1)
    for s in range(sb):
        rows = slice(s * tb, (s + 1) * tb)
        xin = xbc_ref[rows, :]
        act = _silu(_conv_taps(xin, tail_ref[s], cw_ref, cb_ref))
        tail_ref[s] = xin[tb - V7X_SUBLANES:tb]
        zs = z_ref[rows, :]
        for ci in range(tb // c):
            r0 = s * tb + ci * c
            cr = slice(ci * c, (ci + 1) * c)
            xs = act[cr, 0:BRANCH_W]
            bm = act[cr, BRANCH_W:BRANCH_W + SSM_GW]
            cm = act[cr, BRANCH_W + SSM_GW:BRANCH_W + 2 * SSM_GW]
            cum = cum_all[r0:r0 + c, :]
            dt = dt_all[r0:r0 + c, :]
            cumt = cumt_all[:, r0:r0 + c]
            dtt = dtt_all[:, r0:r0 + c]
            ys = []
            for g in range(SSM_GROUPS):
                gs = slice(g * SSM_STATE, (g + 1) * SSM_STATE)
                cmg, bmg = cm[:, gs], bm[:, gs]
                cb_g = _dot_nt(cmg, bmg)
                for hh in range(SSM_HPG):
                    h = g * SSM_HPG + hh
                    xh = xs[:, h * SSM_HEADDIM:(h + 1) * SSM_HEADDIM]
                    ccol = cum[:, h:h + 1]
                    seg = ccol - cumt[h:h + 1, :]
                    dec = jnp.where(causal, jnp.exp(jnp.where(causal, seg, 0.0)), 0.0)
                    att = cb_g * dec * dtt[h:h + 1, :]
                    st = s_ref[s, h]
                    y = _dot(att, xh) + _dot(cmg * jnp.exp(ccol), st)
                    cl = cum[c - 1:c, h:h + 1]
                    wcol = dt[:, h:h + 1] * jnp.exp(cl - ccol)
                    s_ref[s, h] = jnp.exp(cl) * st + _dot_tn(bmg * wcol, xh)
                    ys.append(y)
            y = jnp.concatenate(ys, axis=1) + dvec_ref[...] * xs
            y = y * _silu(zs[cr, :])
            gw = BRANCH_W // SSM_GROUPS
            outs = []
            for g in range(SSM_GROUPS):
                yg = y[:, g * gw:(g + 1) * gw]
                ms = jnp.mean(yg * yg, axis=-1, keepdims=True)
                outs.append(yg * lax.rsqrt(ms + EPS))
            yn = jnp.concatenate(outs, axis=1) * ng_ref[...]
            out_ref[r0:r0 + c, :] = yn.astype(out_ref.dtype)


def _ssd_mixer(proj, dt_t, prev8, s0, cw, cb, dtb, alog, dtbc, alogc, dvec, ng, *, bn, seq, sb, tb):
    nt = seq // tb
    rows = sb * tb
    c = min(HG_CHUNK, tb)
    kern = functools.partial(_ssd_kernel, sb=sb, tb=tb, c=c)
    st_spec = pl.BlockSpec((sb, SSM_HEADS, SSM_STATE, SSM_HEADDIM), lambda b, i: (b, 0, 0, 0))
    tl_spec = pl.BlockSpec((sb, V7X_SUBLANES, SSM_XBC), lambda b, i: (b, 0, 0))
    return pl.pallas_call(
        kern,
        out_shape=(jax.ShapeDtypeStruct((bn * seq, BRANCH_W), BF16),
                   jax.ShapeDtypeStruct((bn, V7X_SUBLANES, SSM_XBC), F32),
                   jax.ShapeDtypeStruct((bn, SSM_HEADS, SSM_STATE, SSM_HEADDIM), F32)),
        grid=(bn // sb, nt),
        in_specs=[pl.BlockSpec((rows, SSM_XBC), lambda b, i: (b * nt + i, COL_XBC // SSM_XBC)),
                  pl.BlockSpec((rows, BRANCH_W), lambda b, i: (b * nt + i, COL_Z // BRANCH_W)),
                  pl.BlockSpec((rows, DT_PAD), lambda b, i: (b * nt + i, COL_DT // DT_PAD)),
                  pl.BlockSpec((SSM_HEADS, rows), lambda b, i: (0, b * nt + i)),
                  tl_spec, st_spec,
                  _full_spec(cw), _full_spec(cb), _full_spec(dtb), _full_spec(alog), _full_spec(dtbc),
                  _full_spec(alogc), _full_spec(dvec), _full_spec(ng)],
        out_specs=(pl.BlockSpec((rows, BRANCH_W), lambda b, i: (b * nt + i, 0)), tl_spec, st_spec),
        compiler_params=_cparams(("arbitrary", "arbitrary")),
        name="ssd_mixer",
    )(proj, proj, proj, dt_t, prev8, s0, cw, cb, dtb, alog, dtbc, alogc, dvec, ng)


def _merge_kernel(x_ref, b0_ref, b1_ref, b2_ref, b3_ref, g0_ref, g1_ref, g2_ref, g3_ref, wb_ref, wo_ref,
                  lg_ref, lb_ref, y_ref, yb_ref):
    acc = None
    for br, gt, k in ((b0_ref, g0_ref, 0), (b1_ref, g1_ref, 1), (b2_ref, g2_ref, 2), (b3_ref, g3_ref, 3)):
        term = _sigmoid(gt[...]) * _dot(br[...], wb_ref[k])
        acc = term if acc is None else acc + term
    mix = _dot(acc.astype(BF16), wo_ref[...])
    y = _layer_norm_rows(ALPHA * x_ref[...] + mix, lg_ref[...], lb_ref[...])
    y_ref[...] = y
    yb_ref[...] = y.astype(BF16)


def _merge(x, brs, proj, wb, wo, lg, lb):
    t = x.shape[0]
    tm = min(512, t)
    row = lambda i: (i, 0)
    gate_specs = [pl.BlockSpec((tm, D_MODEL), functools.partial(lambda i, k: (i, COL_GATE // D_MODEL + k), k=k))
                  for k in range(N_BRANCH)]
    return pl.pallas_call(
        _merge_kernel,
        out_shape=(jax.ShapeDtypeStruct((t, D_MODEL), F32), jax.ShapeDtypeStruct((t, D_MODEL), BF16)),
        grid=(t // tm,),
        in_specs=[pl.BlockSpec((tm, D_MODEL), row)] + [pl.BlockSpec((tm, BRANCH_W), row)] * N_BRANCH + gate_specs
                 + [_full_spec(wb), _full_spec(wo), _full_spec(lg), _full_spec(lb)],
        out_specs=(pl.BlockSpec((tm, D_MODEL), row), pl.BlockSpec((tm, D_MODEL), row)),
        compiler_params=_cparams(("arbitrary",)),
        name="merge_ln",
    )(x, *brs, proj, proj, proj, proj, wb, wo, lg, lb)


PEER_TR = 128
PEER_SLOTS = PEER_HEADS * PEER_TOPK
_CAND_GROUPS = ((0, 16), (1, 8), (2, 5), (3, 4), (4, 3), (5, 2), (6, 2), (7, 2))
PEER_NCAND = 16 + 8 * 7 + 8


def _cand_tables():
    ids = np.zeros((PEER_NCAND,), np.int32)
    msk = np.zeros((PEER_NCAND,), np.float32)
    r = 0
    for a, nb in _CAND_GROUPS:
        width = 16 if a == 0 else 8
        for b in range(width):
            ids[r] = a * PEER_TOPK + b
            msk[r] = 0.0 if b < nb else NEG_INF
            r += 1
    for a in range(8, 16):
        ids[r] = a * PEER_TOPK
        r += 1
    assert r == PEER_NCAND
    return (jnp.asarray(np.repeat(ids[:, None], PEER_TR, axis=1)),
            jnp.asarray(np.repeat(msk[:, None], PEER_TR, axis=1)))


def _top16(s, ids, big):
    vals, idxs = [], []
    for _ in range(PEER_TOPK):
        m = jnp.max(s, axis=0, keepdims=True)
        idx = jnp.min(jnp.where(s == m, ids, big), axis=0, keepdims=True)
        vals.append(m)
        idxs.append(idx)
        s = jnp.where(ids == idx, NEG_INF, s)
    return jnp.concatenate(vals, axis=0), jnp.concatenate(idxs, axis=0)


def _route_kernel(x_ref, wq_ref, keys_ref, cid_ref, cmask_ref, i1_ref, i2_ref, g_ref, q_scr, v_scr, n_scr):
    q_scr[...] = _dot(x_ref[...], wq_ref[...])
    key_ids = _row_iota((N_KEYS, PEER_TR))

    def stage1(j, carry):
        c0 = pl.multiple_of(j * PEER_HALF, PEER_HALF)
        qh = q_scr[:, pl.ds(c0, PEER_HALF)]
        st = _dot_nt(keys_ref[j % 2], qh)
        vals, idxs = _top16(st, key_ids, N_KEYS)
        v_scr[j] = vals
        n_scr[j] = idxs
        return carry

    lax.fori_loop(0, 2 * PEER_HEADS, stage1, 0)

    def stage2(h, carry):
        v1, v2 = v_scr[2 * h], v_scr[2 * h + 1]
        n1, n2 = n_scr[2 * h], n_scr[2 * h + 1]
        rows = [v1[0:1] + v2]
        for a, _ in _CAND_GROUPS[1:]:
            rows.append(v1[a:a + 1] + v2[0:8])
        rows.append(v1[8:16] + v2[0:1])
        cand = jnp.concatenate(rows, axis=0) + cmask_ref[...]
        sc, pos = _top16(cand, cid_ref[...], PEER_TOPK * PEER_TOPK)
        ra = lax.shift_right_logical(pos, 4)
        rb = lax.bitwise_and(pos, PEER_TOPK - 1)
        e1 = jnp.zeros((PEER_TOPK, PEER_TR), I32)
        e2 = jnp.zeros((PEER_TOPK, PEER_TR), I32)
        for r in range(PEER_TOPK):
            e1 = jnp.where(ra == r, n1[r:r + 1], e1)
            e2 = jnp.where(rb == r, n2[r:r + 1], e2)
        ex = jnp.exp(sc - sc[0:1])
        gw = ex / jnp.sum(ex, axis=0, keepdims=True)
        i1_ref[0, pl.ds(pl.multiple_of(h * PEER_TOPK, PEER_TOPK), PEER_TOPK), :] = e1
        i2_ref[0, pl.ds(pl.multiple_of(h * PEER_TOPK, PEER_TOPK), PEER_TOPK), :] = e2
        g_ref[0, pl.ds(pl.multiple_of(h * PEER_TOPK, PEER_TOPK), PEER_TOPK), :] = gw
        return carry

    lax.fori_loop(0, PEER_HEADS, stage2, 0)


def _peer_route(xb, wq, keys, cid, cmask):
    t = xb.shape[0]
    nb = t // PEER_TR
    slot_spec = pl.BlockSpec((1, PEER_SLOTS, PEER_TR), lambda i: (i, 0, 0))
    i1, i2, g = pl.pallas_call(
        _route_kernel,
        out_shape=(jax.ShapeDtypeStruct((nb, PEER_SLOTS, PEER_TR), I32),
                   jax.ShapeDtypeStruct((nb, PEER_SLOTS, PEER_TR), I32),
                   jax.ShapeDtypeStruct((nb, PEER_SLOTS, PEER_TR), F32)),
        grid=(nb,),
        in_specs=[pl.BlockSpec((PEER_TR, D_MODEL), lambda i: (i, 0)),
                  _full_spec(wq), _full_spec(keys), _full_spec(cid), _full_spec(cmask)],
        out_specs=(slot_spec, slot_spec, slot_spec),
        scratch_shapes=[pltpu.VMEM((PEER_TR, PEER_HEADS * PEER_QDIM), F32),
                        pltpu.VMEM((2 * PEER_HEADS, PEER_TOPK, PEER_TR), F32),
                        pltpu.VMEM((2 * PEER_HEADS, PEER_TOPK, PEER_TR), I32)],
        compiler_params=_cparams(("arbitrary",)),
        name="peer_route",
    )(xb, wq, keys, cid, cmask)
    tok = lambda a: jnp.swapaxes(a, 1, 2).reshape(t, PEER_SLOTS)
    return tok(i1), tok(i2), tok(g)


PEER_TB = 256
PEER_ET = 1024
C_PITCH = N_KEYS + 8


def _experts_kernel(xb_ref, x_ref, i1_ref, i2_ref, g_ref, ut_ref, v_ref, lg_ref, lb_ref, y_ref, yb_ref,
                    c_scr, acc_scr, *, tb, et, ne):
    e = pl.program_id(1)

    @pl.when(e == 0)
    def _():
        sub = _row_iota((N_KEYS, PEER_SLOTS))

        def build(t, carry):
            row = pl.ds(t, 1)
            pt = jnp.where(sub == i1_ref[row, :], g_ref[row, :], 0.0).astype(BF16)
            qt = jnp.where(sub == i2_ref[row, :], 1.0, 0.0).astype(BF16)
            c_scr[pl.ds(pl.multiple_of(t * C_PITCH, V7X_SUBLANES), N_KEYS), :] = _dot_nt(pt, qt)
            return carry

        lax.fori_loop(0, tb, build, 0)
        acc_scr[...] = jnp.zeros_like(acc_scr)

    act = _gelu(_dot(xb_ref[...], ut_ref[...]))
    nk = et // N_KEYS
    coef = jnp.concatenate([c_scr[pl.ds(e * nk + j, tb, stride=C_PITCH), :] for j in range(nk)], axis=1)
    acc_scr[...] += _dot((act * coef).astype(BF16), v_ref[...])

    @pl.when(e == ne - 1)
    def _():
        y = _layer_norm_rows(ALPHA * x_ref[...] + acc_scr[...], lg_ref[...], lb_ref[...])
        y_ref[...] = y
        yb_ref[...] = y.astype(BF16)


def _peer_experts(xb, x, i1, i2, g, ut, v, lg, lb):
    t = x.shape[0]
    tb = min(PEER_TB, t)
    et = PEER_ET
    ne = N_EXPERTS // et
    kern = functools.partial(_experts_kernel, tb=tb, et=et, ne=ne)
    row = lambda i, e: (i, 0)
    return pl.pallas_call(
        kern,
        out_shape=(jax.ShapeDtypeStruct((t, D_MODEL), F32), jax.ShapeDtypeStruct((t, D_MODEL), BF16)),
        grid=(t // tb, ne),
        in_specs=[pl.BlockSpec((tb, D_MODEL), row), pl.BlockSpec((tb, D_MODEL), row),
                  pl.BlockSpec((tb, PEER_SLOTS), row), pl.BlockSpec((tb, PEER_SLOTS), row),
                  pl.BlockSpec((tb, PEER_SLOTS), row),
                  pl.BlockSpec((D_MODEL, et), lambda i, e: (0, e)),
                  pl.BlockSpec((et, D_MODEL), lambda i, e: (e, 0)),
                  _full_spec(lg), _full_spec(lb)],
        out_specs=(pl.BlockSpec((tb, D_MODEL), row), pl.BlockSpec((tb, D_MODEL), row)),
        scratch_shapes=[pltpu.VMEM((tb * C_PITCH, N_KEYS), F32), pltpu.VMEM((tb, D_MODEL), F32)],
        compiler_params=_cparams(("arbitrary", "arbitrary")),
        name="peer_experts",
    )(xb, x, i1, i2, g, ut, v, lg, lb)


def _prep_layer(l, w_in, rg_conv_w, rg_conv_b, rg_wa, rg_ba, rg_wx, rg_bx, rg_lambda, ret_norm_g, hg_norm_g,
                ssm_conv_w, ssm_conv_b, ssm_dt_bias, ssm_a_log, ssm_d, ssm_norm_g, w_branch, w_out, ln1_g, ln1_b,
                peer_wq, peer_keys, peer_u, peer_v, ln2_g, ln2_b):
    w = w_in[l]
    z0 = COL_HG + 2048
    x0 = z0 + BRANCH_W
    d0 = x0 + SSM_XBC
    g0 = d0 + SSM_HEADS
    w_k = jnp.concatenate([w[:, :z0], w[:, x0:d0], w[:, g0:], w[:, z0:x0], w[:, d0:g0],
                           jnp.zeros((D_MODEL, DT_PAD - SSM_HEADS), w.dtype)], axis=1).astype(BF16)
    eye = jnp.eye(RG_BLOCKS, dtype=F32)
    bd = lambda m: jnp.einsum('kij,kl->kilj', m, eye).reshape(BRANCH_W, BRANCH_W)
    return dict(
        w_in=w_k,
        rg=(rg_conv_w[l], rg_conv_b[l][None, :],
            jnp.concatenate([bd(rg_wa[l]), bd(rg_wx[l])], axis=1).astype(BF16),
            jnp.concatenate([rg_ba[l], rg_bx[l]])[None, :], rg_lambda[l][None, :]),
        ret_ng=ret_norm_g[l][None, :],
        hg_ng=hg_norm_g[l][None, :],
        ssd=(ssm_conv_w[l], ssm_conv_b[l], ssm_dt_bias[l], ssm_a_log[l], ssm_d[l], ssm_norm_g[l]),
        wb=w_branch[l].astype(BF16), wo=w_out[l].astype(BF16),
        ln1=(ln1_g[l][None, :], ln1_b[l][None, :]),
        wq=peer_wq[l].astype(BF16), keys=peer_keys[l],
        ut=peer_u[l].T.astype(BF16), v=peer_v[l].astype(BF16),
        ln2=(ln2_g[l][None, :], ln2_b[l][None, :]),
    )


def _pad_hist(buf):
    return jnp.pad(buf, ((0, 0), (V7X_SUBLANES - (CONV_W - 1), 0), (0, 0)))


def _layer(l, x, xb, state, lp, hg_lower, ones_bd, cand, cs_tab, *, bn, seq, sb, tb):
    rg_h, rg_buf, ret_s, hg_s, ssm_s, ssm_buf = state
    geo = dict(bn=bn, seq=seq, sb=sb, tb=tb)
    proj = _in_proj(xb, lp['w_in'])
    rg_out, rg_tail, rg_hl = _rg_mixer(proj, _pad_hist(rg_buf), rg_h[:, None, :], *lp['rg'], **geo)
    ret_out, ret_new = _ret_mixer(proj, cs_tab, ret_s, lp['ret_ng'], **geo)
    hg_out, hg_new = _hg_mixer(proj, hg_lower, ones_bd, hg_s, lp['hg_ng'], layer=l, **geo)
    dt_t, cw, cb, dtb, alog, dtbc, alogc, dvec, ng = _ssd_prep(proj, *lp['ssd'])
    ssd_out, ssd_tail, ssm_new = _ssd_mixer(proj, dt_t, _pad_hist(ssm_buf), ssm_s, cw, cb, dtb, alog, dtbc, alogc,
                                            dvec, ng, **geo)
    x1, x1b = _merge(x, (rg_out, ret_out, hg_out, ssd_out), proj, lp['wb'], lp['wo'], *lp['ln1'])
    i1, i2, g = _peer_route(x1b, lp['wq'], lp['keys'], *cand)
    x2, x2b = _peer_experts(x1b, x1, i1, i2, g, lp['ut'], lp['v'], *lp['ln2'])
    hist = slice(V7X_SUBLANES - (CONV_W - 1), V7X_SUBLANES)
    new_state = (rg_hl[:, V7X_SUBLANES - 1], rg_tail[:, hist], ret_new, hg_new, ssm_new, ssd_tail[:, hist])
    return x2, x2b, new_state


def kernel(x_prompt, x_sample, state_rglru_h, state_rglru_conv, state_ret, state_hgrn, state_ssm, state_ssm_conv, w_in, rg_conv_w, rg_conv_b, rg_wa, rg_ba, rg_wx, rg_bx, rg_lambda, ret_norm_g, hg_lower, hg_norm_g, ssm_conv_w, ssm_conv_b, ssm_dt_bias, ssm_a_log, ssm_d, ssm_norm_g, w_branch, w_out, ln1_g, ln1_b, peer_wq, peer_keys, peer_u, peer_v, ln2_g, ln2_b):
    bp, lp_len, _ = x_prompt.shape
    bs, ls_len, _ = x_sample.shape
    sdt = state_ret.dtype
    ones_bd = _block_ones()
    cand = _cand_tables()
    cs_p = _rotary_tables(0.0, lp_len)
    cs_s = _rotary_tables(float(PAST_LEN), ls_len)
    geo_p = dict(bn=bp, seq=lp_len, sb=1, tb=256)
    geo_s = dict(bn=bs, seq=ls_len, sb=16, tb=ls_len)

    xp = x_prompt.reshape(bp * lp_len, D_MODEL)
    xs = x_sample.reshape(bs * ls_len, D_MODEL)
    xpb, xsb = xp.astype(BF16), xs.astype(BF16)
    new_p, new_s = [], []
    for l in range(DEPTH):
        lp = _prep_layer(l, w_in, rg_conv_w, rg_conv_b, rg_wa, rg_ba, rg_wx, rg_bx, rg_lambda, ret_norm_g, hg_norm_g,
                         ssm_conv_w, ssm_conv_b, ssm_dt_bias, ssm_a_log, ssm_d, ssm_norm_g, w_branch, w_out, ln1_g,
                         ln1_b, peer_wq, peer_keys, peer_u, peer_v, ln2_g, ln2_b)
        zero = (jnp.zeros((bp, BRANCH_W), sdt), jnp.zeros((bp, CONV_W - 1, BRANCH_W), sdt),
                jnp.zeros((bp, RET_HEADS, RET_DK, RET_DK), sdt), jnp.zeros((bp, HG_HEADS, HG_DK, HG_DK), sdt),
                jnp.zeros((bp, SSM_HEADS, SSM_STATE, SSM_HEADDIM), sdt), jnp.zeros((bp, CONV_W - 1, SSM_XBC), sdt))
        xp, xpb, sp = _layer(l, xp, xpb, zero, lp, hg_lower, ones_bd, cand, cs_p, **geo_p)
        carried = (state_rglru_h[l], state_rglru_conv[l], state_ret[l], state_hgrn[l], state_ssm[l], state_ssm_conv[l])
        xs, xsb, ss = _layer(l, xs, xsb, carried, lp, hg_lower, ones_bd, cand, cs_s, **geo_s)
        new_p.append(sp)
        new_s.append(ss)
    p_states = [jnp.stack(t) for t in zip(*new_p)]
    s_states = [jnp.stack(t) for t in zip(*new_s)]
    return (xp.reshape(bp, lp_len, D_MODEL), xs.reshape(bs, ls_len, D_MODEL), *p_states, *s_states)
```

```python
import functools
import math

import numpy as np
import jax
import jax.numpy as jnp
from jax import lax
from jax.experimental import pallas as pl
from jax.experimental.pallas import tpu as pltpu

F32 = jnp.float32
BF16 = jnp.bfloat16
I32 = jnp.int32

D_MODEL = 1024
DEPTH = 2
PAST_LEN = 16384
EPS = 1e-5
LB_FLOOR = 1e-20
BRANCH_W = D_MODEL // 2
N_BRANCH = 4
CONV_W = 4
RG_BLOCKS = 8
RG_BLOCK = BRANCH_W // RG_BLOCKS
RG_C = 8.0
RET_HEADS = 4
RET_DK = BRANCH_W // RET_HEADS
ROPE_BASE = 10000.0
HG_HEADS = 4
HG_DK = BRANCH_W // HG_HEADS
SSM_HEADDIM = 64
SSM_HEADS = BRANCH_W // SSM_HEADDIM
SSM_GROUPS = 2
SSM_STATE = 128
SSM_XBC = BRANCH_W + 2 * SSM_GROUPS * SSM_STATE
PEER_HEADS = 8
N_KEYS = 128
N_EXPERTS = N_KEYS * N_KEYS
PEER_TOPK = 16
PEER_QDIM = 256
PEER_HALF = PEER_QDIM // 2
ALPHA = (2.0 * DEPTH) ** 0.25

V7X_LANES = 128
V7X_SUBLANES = 8
V7X_VMEM_LIMIT_BYTES = 56 * 1024 * 1024

COL_RG = 0
COL_RET = 1024
COL_HG = 3072
COL_XBC = 5120
COL_GATE = 6144
COL_Z = 10240
COL_DT = 10752
N_PROJ = 10880
DT_PAD = 128

HG_CHUNK = 64
HG_SUB = 16
NEG_INF = float("-inf")


def _cparams(sem):
    return pltpu.CompilerParams(dimension_semantics=sem, vmem_limit_bytes=V7X_VMEM_LIMIT_BYTES)


def _full_spec(arr):
    zeros = (0,) * arr.ndim
    return pl.BlockSpec(arr.shape, lambda *_: zeros)


def _sigmoid(x):
    return 1.0 / (1.0 + jnp.exp(-x))


def _silu(x):
    return x * _sigmoid(x)


def _softplus(x):
    return jnp.maximum(x, 0.0) + jnp.log(1.0 + jnp.exp(-jnp.abs(x)))


def _gelu(x):
    return 0.5 * x * (1.0 + lax.erf(x * (1.0 / math.sqrt(2.0))))


def _one_minus_exp(x):
    e = jnp.exp(x)
    le = jnp.log(e)
    safe = jnp.where(le == 0.0, 1.0, le)
    return jnp.where(le == 0.0, -x, (1.0 - e) * x / safe)


def _row_iota(shape):
    return lax.broadcasted_iota(I32, shape, 0)


def _shift_rows(x, d, fill):
    rolled = pltpu.roll(x, d, axis=0)
    return jnp.where(_row_iota(x.shape) >= d, rolled, fill)


def _cumsum_rows(x):
    n = x.shape[0]
    d = 1
    while d < n:
        x = x + _shift_rows(x, d, 0.0)
        d *= 2
    return x


def _dot(a, b):
    return jnp.dot(a, b, preferred_element_type=F32)


def _dot_nt(a, b):
    return lax.dot_general(a, b, (((1,), (1,)), ((), ())), preferred_element_type=F32)


def _dot_tn(a, b):
    return lax.dot_general(a, b, (((0,), (0,)), ((), ())), preferred_element_type=F32)


def _layer_norm_rows(y, g, b):
    mu = jnp.mean(y, axis=-1, keepdims=True)
    yc = y - mu
    var = jnp.mean(yc * yc, axis=-1, keepdims=True)
    return yc * lax.rsqrt(var + EPS) * g + b


def _matmul_kernel(x_ref, w_ref, o_ref):
    o_ref[...] = _dot(x_ref[...], w_ref[...])


def _in_proj(xb, w):
    t = xb.shape[0]
    tm = min(512, t)
    tn = N_PROJ // 5
    return pl.pallas_call(
        _matmul_kernel,
        out_shape=jax.ShapeDtypeStruct((t, N_PROJ), F32),
        grid=(N_PROJ // tn, t // tm),
        in_specs=[pl.BlockSpec((tm, D_MODEL), lambda j, i: (i, 0)),
                  pl.BlockSpec((D_MODEL, tn), lambda j, i: (0, j))],
        out_specs=pl.BlockSpec((tm, tn), lambda j, i: (i, j)),
        compiler_params=_cparams(("arbitrary", "arbitrary")),
        name="in_proj",
    )(xb, w)


def _conv_taps(x, prev8, w_ref, b_ref):
    tb = x.shape[0]
    row8 = _row_iota((V7X_SUBLANES, x.shape[1]))
    out = b_ref[...] + x * w_ref[CONV_W - 1:CONV_W, :]
    for k in range(1, CONV_W):
        r = pltpu.roll(x, k, axis=0)
        p = pltpu.roll(prev8, k, axis=0)
        top = jnp.where(row8 >= k, r[0:V7X_SUBLANES], p)
        sh = top if tb == V7X_SUBLANES else jnp.concatenate([top, r[V7X_SUBLANES:]], axis=0)
        out = out + sh * w_ref[CONV_W - 1 - k:CONV_W - k, :]
    return out


def _rg_kernel(xz_ref, prev_ref, h0_ref, cw_ref, cb_ref, wax_ref, bax_ref, lam_ref,
               out_ref, tail_ref, hl_ref, *, sb, tb):
    i = pl.program_id(1)

    @pl.when(i == 0)
    def _():
        tail_ref[...] = prev_ref[...]
        hl_ref[...] = jnp.broadcast_to(h0_ref[...], hl_ref.shape)

    sp8 = -RG_C * _softplus(-lam_ref[...])
    row = _row_iota((tb, BRANCH_W))
    for s in range(sb):
        rows = slice(s * tb, (s + 1) * tb)
        x = xz_ref[rows, 0:BRANCH_W]
        z = xz_ref[rows, BRANCH_W:2 * BRANCH_W]
        u = _conv_taps(x, tail_ref[s], cw_ref, cb_ref)
        gates = _dot(u.astype(BF16), wax_ref[...]) + bax_ref[...]
        r = _sigmoid(gates[:, 0:BRANCH_W])
        ig = _sigmoid(gates[:, BRANCH_W:2 * BRANCH_W])
        log_a = sp8 * r
        a = jnp.exp(log_a)
        b = jnp.sqrt(_one_minus_exp(2.0 * log_a)) * (ig * u)
        d = 1
        while d < tb:
            a_s = jnp.where(row >= d, pltpu.roll(a, d, axis=0), 1.0)
            b_s = jnp.where(row >= d, pltpu.roll(b, d, axis=0), 0.0)
            b = a * b_s + b
            a = a * a_s
            d *= 2
        h = b + a * hl_ref[s, V7X_SUBLANES - 1:V7X_SUBLANES, :]
        out_ref[rows, :] = (h * _gelu(z)).astype(out_ref.dtype)
        tail_ref[s] = x[tb - V7X_SUBLANES:tb]
        hl_ref[s] = h[tb - V7X_SUBLANES:tb]


def _rg_mixer(proj, prev8, h0, cw, cb, wax, bax, lam, *, bn, seq, sb, tb):
    nt = seq // tb
    rows = sb * tb
    kern = functools.partial(_rg_kernel, sb=sb, tb=tb)
    st_spec = pl.BlockSpec((sb, V7X_SUBLANES, BRANCH_W), lambda b, i: (b, 0, 0))
    return pl.pallas_call(
        kern,
        out_shape=(jax.ShapeDtypeStruct((bn * seq, BRANCH_W), BF16),
                   jax.ShapeDtypeStruct((bn, V7X_SUBLANES, BRANCH_W), F32),
                   jax.ShapeDtypeStruct((bn, V7X_SUBLANES, BRANCH_W), F32)),
        grid=(bn // sb, nt),
        in_specs=[pl.BlockSpec((rows, 2 * BRANCH_W), lambda b, i: (b * nt + i, COL_RG // (2 * BRANCH_W))),
                  st_spec,
                  pl.BlockSpec((sb, 1, BRANCH_W), lambda b, i: (b, 0, 0)),
                  _full_spec(cw), _full_spec(cb), _full_spec(wax), _full_spec(bax), _full_spec(lam)],
        out_specs=(pl.BlockSpec((rows, BRANCH_W), lambda b, i: (b * nt + i, 0)), st_spec, st_spec),
        compiler_params=_cparams(("arbitrary", "arbitrary")),
        name="rg_mixer",
    )(proj, prev8, h0, cw, cb, wax, bax, lam)


_RET_LOG_GAMMA = tuple(math.log1p(-2.0 ** (-5.0 - h)) for h in range(RET_HEADS))


def _ret_kernel(qk_ref, vg_ref, cs_ref, s0_ref, ng_ref, out_ref, s_ref, dmat, gq, we, *, sb, tb):
    b = pl.program_id(0)
    i = pl.program_id(1)

    @pl.when((b == 0) & (i == 0))
    def _():
        dif = (lax.broadcasted_iota(I32, (tb, tb), 0) - lax.broadcasted_iota(I32, (tb, tb), 1)).astype(F32)
        rowf = lax.broadcasted_iota(I32, (tb, RET_DK), 0).astype(F32)
        for h in range(RET_HEADS):
            lg = _RET_LOG_GAMMA[h]
            dmat[h] = jnp.where(dif >= 0.0, jnp.exp(jnp.maximum(dif, 0.0) * lg), 0.0)
            gq[h] = jnp.exp((rowf + 1.0) * lg)
            we[h] = jnp.exp((tb - 1.0 - rowf) * lg)

    @pl.when(i == 0)
    def _():
        s_ref[...] = s0_ref[...]

    cos2 = cs_ref[0]
    sin2 = cs_ref[1]
    for s in range(sb):
        rows = slice(s * tb, (s + 1) * tb)
        outs = []
        for h in range(RET_HEADS):
            c0 = h * RET_DK
            qh = qk_ref[rows, c0:c0 + RET_DK]
            kh = qk_ref[rows, BRANCH_W + c0:BRANCH_W + c0 + RET_DK]
            vh = vg_ref[rows, c0:c0 + RET_DK]
            gh = vg_ref[rows, BRANCH_W + c0:BRANCH_W + c0 + RET_DK]
            qr = qh * cos2 + pltpu.roll(qh, RET_DK // 2, axis=1) * sin2
            kr = (kh * cos2 + pltpu.roll(kh, RET_DK // 2, axis=1) * sin2) * (RET_DK ** -0.5)
            st = s_ref[s, h]
            att = _dot_nt(qr, kr) * dmat[h]
            o = _dot(att, vh) + _dot(qr * gq[h], st)
            s_ref[s, h] = math.exp(tb * _RET_LOG_GAMMA[h]) * st + _dot_tn(kr * we[h], vh)
            mu = jnp.mean(o, axis=-1, keepdims=True)
            oc = o - mu
            var = jnp.mean(oc * oc, axis=-1, keepdims=True)
            on = oc * lax.rsqrt(var + EPS) * ng_ref[:, c0:c0 + RET_DK]
            outs.append(_silu(gh) * on)
        out_ref[rows, :] = jnp.concatenate(outs, axis=1).astype(out_ref.dtype)


def _ret_mixer(proj, cs_tab, s0, ng, *, bn, seq, sb, tb):
    nt = seq // tb
    rows = sb * tb
    kern = functools.partial(_ret_kernel, sb=sb, tb=tb)
    st_spec = pl.BlockSpec((sb, RET_HEADS, RET_DK, RET_DK), lambda b, i: (b, 0, 0, 0))
    cb = COL_RET // (2 * BRANCH_W)
    return pl.pallas_call(
        kern,
        out_shape=(jax.ShapeDtypeStruct((bn * seq, BRANCH_W), BF16),
                   jax.ShapeDtypeStruct((bn, RET_HEADS, RET_DK, RET_DK), F32)),
        grid=(bn // sb, nt),
        in_specs=[pl.BlockSpec((rows, 2 * BRANCH_W), lambda b, i: (b * nt + i, cb)),
                  pl.BlockSpec((rows, 2 * BRANCH_W), lambda b, i: (b * nt + i, cb + 1)),
                  pl.BlockSpec((2, tb, RET_DK), lambda b, i: (0, i, 0)),
                  st_spec, _full_spec(ng)],
        out_specs=(pl.BlockSpec((rows, BRANCH_W), lambda b, i: (b * nt + i, 0)), st_spec),
        scratch_shapes=[pltpu.VMEM((RET_HEADS, tb, tb), F32),
                        pltpu.VMEM((RET_HEADS, tb, RET_DK), F32),
                        pltpu.VMEM((RET_HEADS, tb, RET_DK), F32)],
        compiler_params=_cparams(("arbitrary", "arbitrary")),
        name="ret_mixer",
    )(proj, proj, cs_tab, s0, ng)


def _rotary_tables(pos0, seq):
    half = RET_DK // 2
    pos = pos0 + jnp.arange(seq, dtype=F32)
    inv = 1.0 / (ROPE_BASE ** jnp.linspace(0.0, 1.0, half, dtype=F32))
    ang = pos[:, None] * inv
    cos, sin = jnp.cos(ang), jnp.sin(ang)
    return jnp.stack([jnp.concatenate([cos, cos], axis=1), jnp.concatenate([-sin, sin], axis=1)])


def _block_ones():
    blk = np.kron(np.eye(HG_HEADS, dtype=np.float32), np.ones((HG_DK, HG_DK), np.float32))
    return jnp.asarray(blk, BF16)


def _hg_lower(hgl_ref, layer):
    raw = hgl_ref[...]
    e = jnp.exp(raw - jnp.max(raw, axis=0, keepdims=True))
    p = e / jnp.sum(e, axis=0, keepdims=True)
    cum = p[0:1]
    for l in range(1, layer + 1):
        cum = cum + p[l:l + 1]
    return cum - p[0:1]


HG_FACTOR_ROWS = 32
HG_SAFE_SPAN = 60.0


def _hg_intra_bounded(q, kk, vi, cum, ones_ref, c, w):
    rmod = _row_iota((c, BRANCH_W)) % w
    o = _dot((q * kk).astype(BF16), ones_ref[...]) * vi
    for delta in range(1, w):
        dec = jnp.exp(jnp.minimum(cum - pltpu.roll(cum, delta, axis=0), 0.0))
        p = q * dec * pltpu.roll(kk, delta, axis=0)
        rs = _dot(p.astype(BF16), ones_ref[...])
        o = o + jnp.where(rmod >= delta, rs, 0.0) * pltpu.roll(vi, delta, axis=0)
    nsub = c // w
    if nsub > 1:
        off = [jnp.zeros((w, BRANCH_W), F32)]
        for i in range(1, nsub):
            ref_row = cum[i * w - 1:i * w, :]
            qt = q[i * w:(i + 1) * w] * jnp.exp(cum[i * w:(i + 1) * w] - ref_row)
            kt = kk[0:i * w] * jnp.exp(ref_row - cum[0:i * w])
            parts = []
            for h in range(HG_HEADS):
                hs = slice(h * HG_DK, (h + 1) * HG_DK)
                att = _dot_nt(qt[:, hs], kt[:, hs])
                parts.append(_dot(att, vi[0:i * w, hs]))
            off.append(jnp.concatenate(parts, axis=1))
        o = o + jnp.concatenate(off, axis=0)
    return o


def _hg_intra_factored(q, kk, vi, cum, c, w):
    head_of_lane = lax.broadcasted_iota(I32, (w, BRANCH_W), 1) // HG_DK
    outs = []
    for i in range(c // w):
        lo, hi = i * w, (i + 1) * w
        base = cum[lo - 1:lo, :] if i > 0 else jnp.zeros((1, BRANCH_W), F32)
        qt = q[lo:hi] * jnp.exp(cum[lo:hi] - base)
        kt = kk[0:hi] * jnp.exp(base - cum[0:hi])
        q_bd = jnp.concatenate([jnp.where(head_of_lane == h, qt, 0.0) for h in range(HG_HEADS)], axis=0)
        rows = _row_iota((HG_HEADS * w, hi)) % w + lo
        causal = rows >= lax.broadcasted_iota(I32, (HG_HEADS * w, hi), 1)
        att = jnp.where(causal, _dot_nt(q_bd, kt), 0.0)
        full = _dot(att, vi[0:hi, :])
        outs.append(jnp.concatenate(
            [full[h * w:(h + 1) * w, h * HG_DK:(h + 1) * HG_DK] for h in range(HG_HEADS)], axis=1))
    return outs[0] if len(outs) == 1 else jnp.concatenate(outs, axis=0)


def _hg_chunk(q, f, vi, st_refs, o_scr, log_lb, log_1mlb, ones_ref, c, w):
    ls = jnp.minimum(f, 0.0) - jnp.log(1.0 + jnp.exp(-jnp.abs(f)))
    b = log_1mlb + ls
    lf = jnp.maximum(log_lb, b) + jnp.log(1.0 + jnp.exp(-jnp.abs(log_lb - b)))
    kk = _one_minus_exp(lf)
    cum = _cumsum_rows(lf) if c > 1 else lf
    wf = min(HG_FACTOR_ROWS, c)
    span = -cum[wf - 1:wf, :]
    for i in range(1, c // wf):
        span = jnp.maximum(span, cum[i * wf - 1:i * wf, :] - cum[(i + 1) * wf - 1:(i + 1) * wf, :])
    small = jnp.max(span) <= HG_SAFE_SPAN

    @pl.when(small)
    def _():
        o_scr[...] = _hg_intra_factored(q, kk, vi, cum, c, wf)

    @pl.when(jnp.logical_not(small))
    def _():
        o_scr[...] = _hg_intra_bounded(q, kk, vi, cum, ones_ref, c, w)

    o = o_scr[...]
    qg = q * jnp.exp(cum)
    cl = cum[c - 1:c, :]
    kw = kk * jnp.exp(cl - cum)
    dec_end = jnp.exp(cl)
    inter = []
    for h in range(HG_HEADS):
        hs = slice(h * HG_DK, (h + 1) * HG_DK)
        st = st_refs(h)
        inter.append(_dot_nt(qg[:, hs], st[...]))
        st[...] = st[...] * dec_end[:, hs] + _dot_tn(vi[:, hs], kw[:, hs])
    return o + jnp.concatenate(inter, axis=1)


def _hg_kernel(qf_ref, ig_ref, hgl_ref, ones_ref, s0_ref, ng_ref, out_ref, s_ref, o_scr, *, sb, tb, c, w, layer, nt):
    i = pl.program_id(1)

    @pl.when(i == 0)
    def _():
        for s in range(sb):
            for h in range(HG_HEADS):
                s_ref[s, h] = s0_ref[s, h].T

    lower = _hg_lower(hgl_ref, layer)
    log_lb = jnp.log(jnp.maximum(lower, LB_FLOOR))
    log_1mlb = jnp.log(1.0 - lower)
    for s in range(sb):
        def chunk(ci, carry):
            r0 = pl.multiple_of(s * tb + ci * c, V7X_SUBLANES)
            rows = pl.ds(r0, c)
            q = qf_ref[rows, 0:BRANCH_W]
            f = qf_ref[rows, BRANCH_W:2 * BRANCH_W]
            vi = ig_ref[rows, 0:BRANCH_W]
            g = ig_ref[rows, BRANCH_W:2 * BRANCH_W]
            o = _hg_chunk(q, f, vi, lambda h: s_ref.at[s, h], o_scr, log_lb, log_1mlb, ones_ref, c, w)
            outs = []
            for h in range(HG_HEADS):
                hs = slice(h * HG_DK, (h + 1) * HG_DK)
                oh = o[:, hs]
                ms = jnp.mean(oh * oh, axis=-1, keepdims=True)
                outs.append(oh * lax.rsqrt(ms + EPS))
            on = jnp.concatenate(outs, axis=1) * ng_ref[...]
            out_ref[rows, :] = (_silu(g) * on).astype(out_ref.dtype)
            return carry

        if tb == c:
            chunk(0, 0)
        else:
            lax.fori_loop(0, tb // c, chunk, 0)

    @pl.when(i == nt - 1)
    def _():
        for s in range(sb):
            for h in range(HG_HEADS):
                s_ref[s, h] = s_ref[s, h].T


def _hg_mixer(proj, hgl, ones_bd, s0, ng, *, bn, seq, sb, tb, layer):
    nt = seq // tb
    rows = sb * tb
    c = min(HG_CHUNK, tb)
    w = min(HG_SUB, c)
    kern = functools.partial(_hg_kernel, sb=sb, tb=tb, c=c, w=w, layer=layer, nt=nt)
    st_spec = pl.BlockSpec((sb, HG_HEADS, HG_DK, HG_DK), lambda b, i: (b, 0, 0, 0))
    cb = COL_HG // (2 * BRANCH_W)
    return pl.pallas_call(
        kern,
        out_shape=(jax.ShapeDtypeStruct((bn * seq, BRANCH_W), BF16),
                   jax.ShapeDtypeStruct((bn, HG_HEADS, HG_DK, HG_DK), F32)),
        grid=(bn // sb, nt),
        in_specs=[pl.BlockSpec((rows, 2 * BRANCH_W), lambda b, i: (b * nt + i, cb)),
                  pl.BlockSpec((rows, 2 * BRANCH_W), lambda b, i: (b * nt + i, cb + 1)),
                  _full_spec(hgl), _full_spec(ones_bd), st_spec, _full_spec(ng)],
        out_specs=(pl.BlockSpec((rows, BRANCH_W), lambda b, i: (b * nt + i, 0)), st_spec),
        scratch_shapes=[pltpu.VMEM((c, BRANCH_W), F32)],
        compiler_params=_cparams(("arbitrary", "arbitrary")),
        name="hg_mixer",
    )(proj, proj, hgl, ones_bd, s0, ng)


SSM_GW = SSM_GROUPS * SSM_STATE
SSM_HPG = SSM_HEADS // SSM_GROUPS


def _ssd_prep(proj, cw, cb, dt_bias, a_log, dvec, ng):
    dt_t = proj[:, COL_DT:COL_DT + SSM_HEADS].T
    pad = DT_PAD - SSM_HEADS
    dtb = jnp.pad(dt_bias, (0, pad))[None, :]
    alog = jnp.pad(a_log, (0, pad))[None, :]
    return (dt_t, cw, cb[None, :], dtb, alog, dt_bias[:, None], a_log[:, None],
            jnp.repeat(dvec, SSM_HEADDIM)[None, :], ng[None, :])


def _seg_cumsum(x, c, axis):
    pos = lax.broadcasted_iota(I32, x.shape, axis) % c
    d = 1
    while d < c:
        x = x + jnp.where(pos >= d, pltpu.roll(x, d, axis=axis), 0.0)
        d *= 2
    return x


def _ssd_kernel(xbc_ref, z_ref, dt_ref, dtt_ref, prev_ref, s0_ref, cw_ref, cb_ref, dtb_ref, alog_ref, dtbc_ref,
                alogc_ref, dvec_ref, ng_ref, out_ref, tail_ref, s_ref, *, sb, tb, c):
    i = pl.program_id(1)

    @pl.when(i == 0)
    def _():
        tail_ref[...] = prev_ref[...]
        s_ref[...] = s0_ref[...]

    dt_all = _softplus(dt_ref[...] + dtb_ref[...])
    cum_all = _seg_cumsum(-dt_all * jnp.exp(alog_ref[...]), c, 0)
    dtt_all = _softplus(dtt_ref[...] + dtbc_ref[...])
    cumt_all = _seg_cumsum(-dtt_all * jnp.exp(alogc_ref[...]), c, 1)
    causal = lax.broadcasted_iota(I32, (c, c), 0) >= lax.broadcasted_iota(I32, (c, c), 1)
    for s in range(sb):
        rows = slice(s * tb, (s + 1) * tb)
        xin = xbc_ref[rows, :]
        act = _silu(_conv_taps(xin, tail_ref[s], cw_ref, cb_ref))
        tail_ref[s] = xin[tb - V7X_SUBLANES:tb]
        zs = z_ref[rows, :]
        for ci in range(tb // c):
            r0 = s * tb + ci * c
            cr = slice(ci * c, (ci + 1) * c)
            xs = act[cr, 0:BRANCH_W]
            bm = act[cr, BRANCH_W:BRANCH_W + SSM_GW]
            cm = act[cr, BRANCH_W + SSM_GW:BRANCH_W + 2 * SSM_GW]
            cum = cum_all[r0:r0 + c, :]
            dt = dt_all[r0:r0 + c, :]
            cumt = cumt_all[:, r0:r0 + c]
            dtt = dtt_all[:, r0:r0 + c]
            ys = []
            for g in range(SSM_GROUPS):
                gs = slice(g * SSM_STATE, (g + 1) * SSM_STATE)
                cmg, bmg = cm[:, gs], bm[:, gs]
                cb_g = _dot_nt(cmg, bmg)
                for hh in range(SSM_HPG):
                    h = g * SSM_HPG + hh
                    xh = xs[:, h * SSM_HEADDIM:(h + 1) * SSM_HEADDIM]
                    ccol = cum[:, h:h + 1]
                    seg = ccol - cumt[h:h + 1, :]
                    dec = jnp.where(causal, jnp.exp(jnp.where(causal, seg, 0.0)), 0.0)
                    att = cb_g * dec * dtt[h:h + 1, :]
                    st = s_ref[s, h]
                    y = _dot(att, xh) + _dot(cmg * jnp.exp(ccol), st)
                    cl = cum[c - 1:c, h:h + 1]
                    wcol = dt[:, h:h + 1] * jnp.exp(cl - ccol)
                    s_ref[s, h] = jnp.exp(cl) * st + _dot_tn(bmg * wcol, xh)
                    ys.append(y)
            y = jnp.concatenate(ys, axis=1) + dvec_ref[...] * xs
            y = y * _silu(zs[cr, :])
            gw = BRANCH_W // SSM_GROUPS
            outs = []
            for g in range(SSM_GROUPS):
                yg = y[:, g * gw:(g + 1) * gw]
                ms = jnp.mean(yg * yg, axis=-1, keepdims=True)
                outs.append(yg * lax.rsqrt(ms + EPS))
            yn = jnp.concatenate(outs, axis=1) * ng_ref[...]
            out_ref[r0:r0 + c, :] = yn.astype(out_ref.dtype)


def _ssd_mixer(proj, dt_t, prev8, s0, cw, cb, dtb, alog, dtbc, alogc, dvec, ng, *, bn, seq, sb, tb):
    nt = seq // tb
    rows = sb * tb
    c = min(HG_CHUNK, tb)
    kern = functools.partial(_ssd_kernel, sb=sb, tb=tb, c=c)
    st_spec = pl.BlockSpec((sb, SSM_HEADS, SSM_STATE, SSM_HEADDIM), lambda b, i: (b, 0, 0, 0))
    tl_spec = pl.BlockSpec((sb, V7X_SUBLANES, SSM_XBC), lambda b, i: (b, 0, 0))
    return pl.pallas_call(
        kern,
        out_shape=(jax.ShapeDtypeStruct((bn * seq, BRANCH_W), BF16),
                   jax.ShapeDtypeStruct((bn, V7X_SUBLANES, SSM_XBC), F32),
                   jax.ShapeDtypeStruct((bn, SSM_HEADS, SSM_STATE, SSM_HEADDIM), F32)),
        grid=(bn // sb, nt),
        in_specs=[pl.BlockSpec((rows, SSM_XBC), lambda b, i: (b * nt + i, COL_XBC // SSM_XBC)),
                  pl.BlockSpec((rows, BRANCH_W), lambda b, i: (b * nt + i, COL_Z // BRANCH_W)),
                  pl.BlockSpec((rows, DT_PAD), lambda b, i: (b * nt + i, COL_DT // DT_PAD)),
                  pl.BlockSpec((SSM_HEADS, rows), lambda b, i: (0, b * nt + i)),
                  tl_spec, st_spec,
                  _full_spec(cw), _full_spec(cb), _full_spec(dtb), _full_spec(alog), _full_spec(dtbc),
                  _full_spec(alogc), _full_spec(dvec), _full_spec(ng)],
        out_specs=(pl.BlockSpec((rows, BRANCH_W), lambda b, i: (b * nt + i, 0)), tl_spec, st_spec),
        compiler_params=_cparams(("arbitrary", "arbitrary")),
        name="ssd_mixer",
    )(proj, proj, proj, dt_t, prev8, s0, cw, cb, dtb, alog, dtbc, alogc, dvec, ng)


def _merge_kernel(x_ref, b0_ref, b1_ref, b2_ref, b3_ref, g0_ref, g1_ref, g2_ref, g3_ref, wb_ref, wo_ref,
                  lg_ref, lb_ref, y_ref, yb_ref):
    acc = None
    for br, gt, k in ((b0_ref, g0_ref, 0), (b1_ref, g1_ref, 1), (b2_ref, g2_ref, 2), (b3_ref, g3_ref, 3)):
        term = _sigmoid(gt[...]) * _dot(br[...], wb_ref[k])
        acc = term if acc is None else acc + term
    mix = _dot(acc.astype(BF16), wo_ref[...])
    y = _layer_norm_rows(ALPHA * x_ref[...] + mix, lg_ref[...], lb_ref[...])
    y_ref[...] = y
    yb_ref[...] = y.astype(BF16)


def _merge(x, brs, proj, wb, wo, lg, lb):
    t = x.shape[0]
    tm = min(512, t)
    row = lambda i: (i, 0)
    gate_specs = [pl.BlockSpec((tm, D_MODEL), functools.partial(lambda i, k: (i, COL_GATE // D_MODEL + k), k=k))
                  for k in range(N_BRANCH)]
    return pl.pallas_call(
        _merge_kernel,
        out_shape=(jax.ShapeDtypeStruct((t, D_MODEL), F32), jax.ShapeDtypeStruct((t, D_MODEL), BF16)),
        grid=(t // tm,),
        in_specs=[pl.BlockSpec((tm, D_MODEL), row)] + [pl.BlockSpec((tm, BRANCH_W), row)] * N_BRANCH + gate_specs
                 + [_full_spec(wb), _full_spec(wo), _full_spec(lg), _full_spec(lb)],
        out_specs=(pl.BlockSpec((tm, D_MODEL), row), pl.BlockSpec((tm, D_MODEL), row)),
        compiler_params=_cparams(("arbitrary",)),
        name="merge_ln",
    )(x, *brs, proj, proj, proj, proj, wb, wo, lg, lb)


PEER_TR = 128
PEER_SLOTS = PEER_HEADS * PEER_TOPK
_CAND_GROUPS = ((0, 16), (1, 8), (2, 5), (3, 4), (4, 3), (5, 2), (6, 2), (7, 2))
PEER_NCAND = 16 + 8 * 7 + 8


def _cand_tables():
    ids = np.zeros((PEER_NCAND,), np.int32)
    msk = np.zeros((PEER_NCAND,), np.float32)
    r = 0
    for a, nb in _CAND_GROUPS:
        width = 16 if a == 0 else 8
        for b in range(width):
            ids[r] = a * PEER_TOPK + b
            msk[r] = 0.0 if b < nb else NEG_INF
            r += 1
    for a in range(8, 16):
        ids[r] = a * PEER_TOPK
        r += 1
    assert r == PEER_NCAND
    return (jnp.asarray(np.repeat(ids[:, None], PEER_TR, axis=1)),
            jnp.asarray(np.repeat(msk[:, None], PEER_TR, axis=1)))


def _top16(s, ids, big):
    vals, idxs = [], []
    for _ in range(PEER_TOPK):
        m = jnp.max(s, axis=0, keepdims=True)
        idx = jnp.min(jnp.where(s == m, ids, big), axis=0, keepdims=True)
        vals.append(m)
        idxs.append(idx)
        s = jnp.where(ids == idx, NEG_INF, s)
    return jnp.concatenate(vals, axis=0), jnp.concatenate(idxs, axis=0)


def _route_kernel(x_ref, wq_ref, keys_ref, cid_ref, cmask_ref, i1_ref, i2_ref, g_ref, q_scr, v_scr, n_scr):
    q_scr[...] = _dot(x_ref[...], wq_ref[...])
    key_ids = _row_iota((N_KEYS, PEER_TR))

    def stage1(j, carry):
        c0 = pl.multiple_of(j * PEER_HALF, PEER_HALF)
        qh = q_scr[:, pl.ds(c0, PEER_HALF)]
        st = _dot_nt(keys_ref[j % 2], qh)
        vals, idxs = _top16(st, key_ids, N_KEYS)
        v_scr[j] = vals
        n_scr[j] = idxs
        return carry

    lax.fori_loop(0, 2 * PEER_HEADS, stage1, 0, unroll=4)

    def stage2(h, carry):
        v1, v2 = v_scr[2 * h], v_scr[2 * h + 1]
        n1, n2 = n_scr[2 * h], n_scr[2 * h + 1]
        rows = [v1[0:1] + v2]
        for a, _ in _CAND_GROUPS[1:]:
            rows.append(v1[a:a + 1] + v2[0:8])
        rows.append(v1[8:16] + v2[0:1])
        cand = jnp.concatenate(rows, axis=0) + cmask_ref[...]
        sc, pos = _top16(cand, cid_ref[...], PEER_TOPK * PEER_TOPK)
        ra = lax.shift_right_logical(pos, 4)
        rb = lax.bitwise_and(pos, PEER_TOPK - 1)
        e1 = jnp.zeros((PEER_TOPK, PEER_TR), I32)
        e2 = jnp.zeros((PEER_TOPK, PEER_TR), I32)
        for r in range(PEER_TOPK):
            e1 = jnp.where(ra == r, n1[r:r + 1], e1)
            e2 = jnp.where(rb == r, n2[r:r + 1], e2)
        ex = jnp.exp(sc - sc[0:1])
        gw = ex / jnp.sum(ex, axis=0, keepdims=True)
        i1_ref[0, pl.ds(pl.multiple_of(h * PEER_TOPK, PEER_TOPK), PEER_TOPK), :] = e1
        i2_ref[0, pl.ds(pl.multiple_of(h * PEER_TOPK, PEER_TOPK), PEER_TOPK), :] = e2
        g_ref[0, pl.ds(pl.multiple_of(h * PEER_TOPK, PEER_TOPK), PEER_TOPK), :] = gw
        return carry

    lax.fori_loop(0, PEER_HEADS, stage2, 0, unroll=4)


def _peer_route(xb, wq, keys, cid, cmask):
    t = xb.shape[0]
    nb = t // PEER_TR
    slot_spec = pl.BlockSpec((1, PEER_SLOTS, PEER_TR), lambda i: (i, 0, 0))
    i1, i2, g = pl.pallas_call(
        _route_kernel,
        out_shape=(jax.ShapeDtypeStruct((nb, PEER_SLOTS, PEER_TR), I32),
                   jax.ShapeDtypeStruct((nb, PEER_SLOTS, PEER_TR), I32),
                   jax.ShapeDtypeStruct((nb, PEER_SLOTS, PEER_TR), F32)),
        grid=(nb,),
        in_specs=[pl.BlockSpec((PEER_TR, D_MODEL), lambda i: (i, 0)),
                  _full_spec(wq), _full_spec(keys), _full_spec(cid), _full_spec(cmask)],
        out_specs=(slot_spec, slot_spec, slot_spec),
        scratch_shapes=[pltpu.VMEM((PEER_TR, PEER_HEADS * PEER_QDIM), F32),
                        pltpu.VMEM((2 * PEER_HEADS, PEER_TOPK, PEER_TR), F32),
                        pltpu.VMEM((2 * PEER_HEADS, PEER_TOPK, PEER_TR), I32)],
        compiler_params=_cparams(("arbitrary",)),
        name="peer_route",
    )(xb, wq, keys, cid, cmask)
    tok = lambda a: jnp.swapaxes(a, 1, 2).reshape(t, PEER_SLOTS)
    return tok(i1), tok(i2), tok(g)


PEER_TB = 512
PEER_ET = 1024
PEER_EC = 256
C_HALF = N_KEYS // 2
C_PITCH = C_HALF + 8
U32 = jnp.uint32
HI16 = 0xFFFF0000


def _experts_kernel(xb_ref, x_ref, i1_ref, i2_ref, g_ref, ut_ref, v_ref, lg_ref, lb_ref, y_ref, yb_ref,
                    c_scr, acc_scr, *, tb, et, ne):
    e = pl.program_id(1)

    @pl.when(e == 0)
    def _():
        sub = _row_iota((N_KEYS, PEER_SLOTS))

        def build(t, carry):
            row = pl.ds(t, 1)
            pt = jnp.where(sub == i1_ref[row, :], g_ref[row, :], 0.0).astype(BF16)
            qt = jnp.where(sub == i2_ref[row, :], 1.0, 0.0).astype(BF16)
            c = _dot_nt(pt, qt).astype(BF16).astype(F32)
            hi = lax.bitcast_convert_type(c[0:C_HALF], U32)
            lo = lax.shift_right_logical(lax.bitcast_convert_type(c[C_HALF:N_KEYS], U32), jnp.uint32(16))
            c_scr[pl.ds(pl.multiple_of(t * C_PITCH, V7X_SUBLANES), C_HALF), :] = hi | lo
            return carry

        lax.fori_loop(0, tb, build, 0, unroll=8)

    nk = et // N_KEYS
    steps_per_half = C_HALF // nk
    row0 = (e % steps_per_half) * nk
    shift = ((e // steps_per_half) * 16).astype(U32)
    per_chunk = PEER_EC // N_KEYS
    part = None
    for c in range(et // PEER_EC):
        cols = slice(c * PEER_EC, (c + 1) * PEER_EC)
        act = _gelu(_dot(xb_ref[...], ut_ref[:, cols]))
        words = jnp.concatenate(
            [c_scr[pl.ds(row0 + c * per_chunk + j, tb, stride=C_PITCH), :] for j in range(per_chunk)], axis=1)
        coef = lax.bitcast_convert_type(jnp.left_shift(words, shift) & jnp.uint32(HI16), F32)
        term = _dot((act * coef).astype(BF16), v_ref[cols, :])
        part = term if part is None else part + term

    @pl.when(e == 0)
    def _():
        acc_scr[...] = part

    @pl.when(e > 0)
    def _():
        acc_scr[...] += part

    @pl.when(e == ne - 1)
    def _():
        y = _layer_norm_rows(ALPHA * x_ref[...] + acc_scr[...], lg_ref[...], lb_ref[...])
        y_ref[...] = y
        yb_ref[...] = y.astype(BF16)


def _peer_experts(xb, x, i1, i2, g, ut, v, lg, lb):
    t = x.shape[0]
    tb = min(PEER_TB, t)
    et = PEER_ET
    ne = N_EXPERTS // et
    kern = functools.partial(_experts_kernel, tb=tb, et=et, ne=ne)
    row = lambda i, e: (i, 0)
    return pl.pallas_call(
        kern,
        out_shape=(jax.ShapeDtypeStruct((t, D_MODEL), F32), jax.ShapeDtypeStruct((t, D_MODEL), BF16)),
        grid=(t // tb, ne),
        in_specs=[pl.BlockSpec((tb, D_MODEL), row), pl.BlockSpec((tb, D_MODEL), row),
                  pl.BlockSpec((tb, PEER_SLOTS), row), pl.BlockSpec((tb, PEER_SLOTS), row),
                  pl.BlockSpec((tb, PEER_SLOTS), row),
                  pl.BlockSpec((D_MODEL, et), lambda i, e: (0, e)),
                  pl.BlockSpec((et, D_MODEL), lambda i, e: (e, 0)),
                  _full_spec(lg), _full_spec(lb)],
        out_specs=(pl.BlockSpec((tb, D_MODEL), row), pl.BlockSpec((tb, D_MODEL), row)),
        scratch_shapes=[pltpu.VMEM((tb * C_PITCH, N_KEYS), U32), pltpu.VMEM((tb, D_MODEL), F32)],
        compiler_params=_cparams(("arbitrary", "arbitrary")),
        name="peer_experts",
    )(xb, x, i1, i2, g, ut, v, lg, lb)


def _prep_layer(l, w_in, rg_conv_w, rg_conv_b, rg_wa, rg_ba, rg_wx, rg_bx, rg_lambda, ret_norm_g, hg_norm_g,
                ssm_conv_w, ssm_conv_b, ssm_dt_bias, ssm_a_log, ssm_d, ssm_norm_g, w_branch, w_out, ln1_g, ln1_b,
                peer_wq, peer_keys, peer_u, peer_v, ln2_g, ln2_b):
    w = w_in[l]
    z0 = COL_HG + 2048
    x0 = z0 + BRANCH_W
    d0 = x0 + SSM_XBC
    g0 = d0 + SSM_HEADS
    w_k = jnp.concatenate([w[:, :z0], w[:, x0:d0], w[:, g0:], w[:, z0:x0], w[:, d0:g0],
                           jnp.zeros((D_MODEL, DT_PAD - SSM_HEADS), w.dtype)], axis=1).astype(BF16)
    eye = jnp.eye(RG_BLOCKS, dtype=F32)
    bd = lambda m: jnp.einsum('kij,kl->kilj', m, eye).reshape(BRANCH_W, BRANCH_W)
    return dict(
        w_in=w_k,
        rg=(rg_conv_w[l], rg_conv_b[l][None, :],
            jnp.concatenate([bd(rg_wa[l]), bd(rg_wx[l])], axis=1).astype(BF16),
            jnp.concatenate([rg_ba[l], rg_bx[l]])[None, :], rg_lambda[l][None, :]),
        ret_ng=ret_norm_g[l][None, :],
        hg_ng=hg_norm_g[l][None, :],
        ssd=(ssm_conv_w[l], ssm_conv_b[l], ssm_dt_bias[l], ssm_a_log[l], ssm_d[l], ssm_norm_g[l]),
        wb=w_branch[l].astype(BF16), wo=w_out[l].astype(BF16),
        ln1=(ln1_g[l][None, :], ln1_b[l][None, :]),
        wq=peer_wq[l].astype(BF16), keys=peer_keys[l],
        ut=peer_u[l].T.astype(BF16), v=peer_v[l].astype(BF16),
        ln2=(ln2_g[l][None, :], ln2_b[l][None, :]),
    )


def _pad_hist(buf):
    return jnp.pad(buf, ((0, 0), (V7X_SUBLANES - (CONV_W - 1), 0), (0, 0)))


def _layer(l, x, xb, state, lp, hg_lower, ones_bd, cand, cs_tab, *, bn, seq, sb, tb):
    rg_h, rg_buf, ret_s, hg_s, ssm_s, ssm_buf = state
    geo = dict(bn=bn, seq=seq, sb=sb, tb=tb)
    proj = _in_proj(xb, lp['w_in'])
    rg_out, rg_tail, rg_hl = _rg_mixer(proj, _pad_hist(rg_buf), rg_h[:, None, :], *lp['rg'], **geo)
    ret_out, ret_new = _ret_mixer(proj, cs_tab, ret_s, lp['ret_ng'], **geo)
    hg_out, hg_new = _hg_mixer(proj, hg_lower, ones_bd, hg_s, lp['hg_ng'], layer=l, **geo)
    dt_t, cw, cb, dtb, alog, dtbc, alogc, dvec, ng = _ssd_prep(proj, *lp['ssd'])
    ssd_out, ssd_tail, ssm_new = _ssd_mixer(proj, dt_t, _pad_hist(ssm_buf), ssm_s, cw, cb, dtb, alog, dtbc, alogc,
                                            dvec, ng, **geo)
    x1, x1b = _merge(x, (rg_out, ret_out, hg_out, ssd_out), proj, lp['wb'], lp['wo'], *lp['ln1'])
    i1, i2, g = _peer_route(x1b, lp['wq'], lp['keys'], *cand)
    x2, x2b = _peer_experts(x1b, x1, i1, i2, g, lp['ut'], lp['v'], *lp['ln2'])
    hist = slice(V7X_SUBLANES - (CONV_W - 1), V7X_SUBLANES)
    new_state = (rg_hl[:, V7X_SUBLANES - 1], rg_tail[:, hist], ret_new, hg_new, ssm_new, ssd_tail[:, hist])
    return x2, x2b, new_state


def kernel(x_prompt, x_sample, state_rglru_h, state_rglru_conv, state_ret, state_hgrn, state_ssm, state_ssm_conv, w_in, rg_conv_w, rg_conv_b, rg_wa, rg_ba, rg_wx, rg_bx, rg_lambda, ret_norm_g, hg_lower, hg_norm_g, ssm_conv_w, ssm_conv_b, ssm_dt_bias, ssm_a_log, ssm_d, ssm_norm_g, w_branch, w_out, ln1_g, ln1_b, peer_wq, peer_keys, peer_u, peer_v, ln2_g, ln2_b):
    bp, lp_len, _ = x_prompt.shape
    bs, ls_len, _ = x_sample.shape
    sdt = state_ret.dtype
    ones_bd = _block_ones()
    cand = _cand_tables()
    cs_p = _rotary_tables(0.0, lp_len)
    cs_s = _rotary_tables(float(PAST_LEN), ls_len)
    geo_p = dict(bn=bp, seq=lp_len, sb=1, tb=256)
    geo_s = dict(bn=bs, seq=ls_len, sb=16, tb=ls_len)

    xp = x_prompt.reshape(bp * lp_len, D_MODEL)
    xs = x_sample.reshape(bs * ls_len, D_MODEL)
    xpb, xsb = xp.astype(BF16), xs.astype(BF16)
    new_p, new_s = [], []
    for l in range(DEPTH):
        lp = _prep_layer(l, w_in, rg_conv_w, rg_conv_b, rg_wa, rg_ba, rg_wx, rg_bx, rg_lambda, ret_norm_g, hg_norm_g,
                         ssm_conv_w, ssm_conv_b, ssm_dt_bias, ssm_a_log, ssm_d, ssm_norm_g, w_branch, w_out, ln1_g,
                         ln1_b, peer_wq, peer_keys, peer_u, peer_v, ln2_g, ln2_b)
        zero = (jnp.zeros((bp, BRANCH_W), sdt), jnp.zeros((bp, CONV_W - 1, BRANCH_W), sdt),
                jnp.zeros((bp, RET_HEADS, RET_DK, RET_DK), sdt), jnp.zeros((bp, HG_HEADS, HG_DK, HG_DK), sdt),
                jnp.zeros((bp, SSM_HEADS, SSM_STATE, SSM_HEADDIM), sdt), jnp.zeros((bp, CONV_W - 1, SSM_XBC), sdt))
        xp, xpb, sp = _layer(l, xp, xpb, zero, lp, hg_lower, ones_bd, cand, cs_p, **geo_p)
        carried = (state_rglru_h[l], state_rglru_conv[l], state_ret[l], state_hgrn[l], state_ssm[l], state_ssm_conv[l])
        xs, xsb, ss = _layer(l, xs, xsb, carried, lp, hg_lower, ones_bd, cand, cs_s, **geo_s)
        new_p.append(sp)
        new_s.append(ss)
    p_states = [jnp.stack(t) for t in zip(*new_p)]
    s_states = [jnp.stack(t) for t in zip(*new_s)]
    return (xp.reshape(bp, lp_len, D_MODEL), xs.reshape(bs, ls_len, D_MODEL), *p_states, *s_states)
```

```python
import functools
import math

import numpy as np
import jax
import jax.numpy as jnp
from jax import lax
from jax.experimental import pallas as pl
from jax.experimental.pallas import tpu as pltpu

F32 = jnp.float32
BF16 = jnp.bfloat16
I32 = jnp.int32

D_MODEL = 1024
DEPTH = 2
PAST_LEN = 16384
EPS = 1e-5
LB_FLOOR = 1e-20
BRANCH_W = D_MODEL // 2
N_BRANCH = 4
CONV_W = 4
RG_BLOCKS = 8
RG_BLOCK = BRANCH_W // RG_BLOCKS
RG_C = 8.0
RET_HEADS = 4
RET_DK = BRANCH_W // RET_HEADS
ROPE_BASE = 10000.0
HG_HEADS = 4
HG_DK = BRANCH_W // HG_HEADS
SSM_HEADDIM = 64
SSM_HEADS = BRANCH_W // SSM_HEADDIM
SSM_GROUPS = 2
SSM_STATE = 128
SSM_XBC = BRANCH_W + 2 * SSM_GROUPS * SSM_STATE
PEER_HEADS = 8
N_KEYS = 128
N_EXPERTS = N_KEYS * N_KEYS
PEER_TOPK = 16
PEER_QDIM = 256
PEER_HALF = PEER_QDIM // 2
ALPHA = (2.0 * DEPTH) ** 0.25

V7X_LANES = 128
V7X_SUBLANES = 8
V7X_VMEM_LIMIT_BYTES = 56 * 1024 * 1024

COL_RG = 0
COL_RET = 1024
COL_HG = 3072
COL_XBC = 5120
COL_GATE = 6144
COL_Z = 10240
COL_DT = 10752
N_PROJ = 10880
DT_PAD = 128

HG_CHUNK = 64
HG_SUB = 16
NEG_INF = float("-inf")


def _cparams(sem):
    return pltpu.CompilerParams(dimension_semantics=sem, vmem_limit_bytes=V7X_VMEM_LIMIT_BYTES)


def _full_spec(arr):
    zeros = (0,) * arr.ndim
    return pl.BlockSpec(arr.shape, lambda *_: zeros)


def _sigmoid(x):
    return 1.0 / (1.0 + jnp.exp(-x))


def _silu(x):
    return x * _sigmoid(x)


def _softplus(x):
    return jnp.maximum(x, 0.0) + jnp.log(1.0 + jnp.exp(-jnp.abs(x)))


def _gelu(x):
    return 0.5 * x * (1.0 + lax.erf(x * (1.0 / math.sqrt(2.0))))


def _one_minus_exp(x):
    e = jnp.exp(x)
    le = jnp.log(e)
    safe = jnp.where(le == 0.0, 1.0, le)
    return jnp.where(le == 0.0, -x, (1.0 - e) * x / safe)


def _row_iota(shape):
    return lax.broadcasted_iota(I32, shape, 0)


def _shift_rows(x, d, fill):
    rolled = pltpu.roll(x, d, axis=0)
    return jnp.where(_row_iota(x.shape) >= d, rolled, fill)


def _cumsum_rows(x):
    n = x.shape[0]
    d = 1
    while d < n:
        x = x + _shift_rows(x, d, 0.0)
        d *= 2
    return x


def _dot(a, b):
    return jnp.dot(a, b, preferred_element_type=F32)


def _dot_nt(a, b):
    return lax.dot_general(a, b, (((1,), (1,)), ((), ())), preferred_element_type=F32)


def _dot_tn(a, b):
    return lax.dot_general(a, b, (((0,), (0,)), ((), ())), preferred_element_type=F32)


def _layer_norm_rows(y, g, b):
    mu = jnp.mean(y, axis=-1, keepdims=True)
    yc = y - mu
    var = jnp.mean(yc * yc, axis=-1, keepdims=True)
    return yc * lax.rsqrt(var + EPS) * g + b


def _matmul_kernel(x_ref, w_ref, o_ref):
    o_ref[...] = _dot(x_ref[...], w_ref[...])


def _in_proj(xb, w):
    t = xb.shape[0]
    tm = min(512, t)
    tn = N_PROJ // 5
    return pl.pallas_call(
        _matmul_kernel,
        out_shape=jax.ShapeDtypeStruct((t, N_PROJ), F32),
        grid=(N_PROJ // tn, t // tm),
        in_specs=[pl.BlockSpec((tm, D_MODEL), lambda j, i: (i, 0)),
                  pl.BlockSpec((D_MODEL, tn), lambda j, i: (0, j))],
        out_specs=pl.BlockSpec((tm, tn), lambda j, i: (i, j)),
        compiler_params=_cparams(("arbitrary", "arbitrary")),
        name="in_proj",
    )(xb, w)


def _conv_taps(x, prev8, w_ref, b_ref):
    tb = x.shape[0]
    row8 = _row_iota((V7X_SUBLANES, x.shape[1]))
    out = b_ref[...] + x * w_ref[CONV_W - 1:CONV_W, :]
    for k in range(1, CONV_W):
        r = pltpu.roll(x, k, axis=0)
        p = pltpu.roll(prev8, k, axis=0)
        top = jnp.where(row8 >= k, r[0:V7X_SUBLANES], p)
        sh = top if tb == V7X_SUBLANES else jnp.concatenate([top, r[V7X_SUBLANES:]], axis=0)
        out = out + sh * w_ref[CONV_W - 1 - k:CONV_W - k, :]
    return out


def _rg_kernel(xz_ref, prev_ref, h0_ref, cw_ref, cb_ref, wax_ref, bax_ref, lam_ref,
               out_ref, tail_ref, hl_ref, *, sb, tb):
    i = pl.program_id(1)

    @pl.when(i == 0)
    def _():
        tail_ref[...] = prev_ref[...]
        hl_ref[...] = jnp.broadcast_to(h0_ref[...], hl_ref.shape)

    sp8 = -RG_C * _softplus(-lam_ref[...])
    row = _row_iota((tb, BRANCH_W))
    for s in range(sb):
        rows = slice(s * tb, (s + 1) * tb)
        x = xz_ref[rows, 0:BRANCH_W]
        z = xz_ref[rows, BRANCH_W:2 * BRANCH_W]
        u = _conv_taps(x, tail_ref[s], cw_ref, cb_ref)
        gates = _dot(u.astype(BF16), wax_ref[...]) + bax_ref[...]
        r = _sigmoid(gates[:, 0:BRANCH_W])
        ig = _sigmoid(gates[:, BRANCH_W:2 * BRANCH_W])
        log_a = sp8 * r
        a = jnp.exp(log_a)
        b = jnp.sqrt(_one_minus_exp(2.0 * log_a)) * (ig * u)
        d = 1
        while d < tb:
            a_s = jnp.where(row >= d, pltpu.roll(a, d, axis=0), 1.0)
            b_s = jnp.where(row >= d, pltpu.roll(b, d, axis=0), 0.0)
            b = a * b_s + b
            a = a * a_s
            d *= 2
        h = b + a * hl_ref[s, V7X_SUBLANES - 1:V7X_SUBLANES, :]
        out_ref[rows, :] = (h * _gelu(z)).astype(out_ref.dtype)
        tail_ref[s] = x[tb - V7X_SUBLANES:tb]
        hl_ref[s] = h[tb - V7X_SUBLANES:tb]


def _rg_mixer(proj, prev8, h0, cw, cb, wax, bax, lam, *, bn, seq, sb, tb):
    nt = seq // tb
    rows = sb * tb
    kern = functools.partial(_rg_kernel, sb=sb, tb=tb)
    st_spec = pl.BlockSpec((sb, V7X_SUBLANES, BRANCH_W), lambda b, i: (b, 0, 0))
    return pl.pallas_call(
        kern,
        out_shape=(jax.ShapeDtypeStruct((bn * seq, BRANCH_W), BF16),
                   jax.ShapeDtypeStruct((bn, V7X_SUBLANES, BRANCH_W), F32),
                   jax.ShapeDtypeStruct((bn, V7X_SUBLANES, BRANCH_W), F32)),
        grid=(bn // sb, nt),
        in_specs=[pl.BlockSpec((rows, 2 * BRANCH_W), lambda b, i: (b * nt + i, COL_RG // (2 * BRANCH_W))),
                  st_spec,
                  pl.BlockSpec((sb, 1, BRANCH_W), lambda b, i: (b, 0, 0)),
                  _full_spec(cw), _full_spec(cb), _full_spec(wax), _full_spec(bax), _full_spec(lam)],
        out_specs=(pl.BlockSpec((rows, BRANCH_W), lambda b, i: (b * nt + i, 0)), st_spec, st_spec),
        compiler_params=_cparams(("arbitrary", "arbitrary")),
        name="rg_mixer",
    )(proj, prev8, h0, cw, cb, wax, bax, lam)


_RET_LOG_GAMMA = tuple(math.log1p(-2.0 ** (-5.0 - h)) for h in range(RET_HEADS))


def _ret_kernel(qk_ref, vg_ref, cs_ref, s0_ref, ng_ref, out_ref, s_ref, dmat, gq, we, *, sb, tb):
    b = pl.program_id(0)
    i = pl.program_id(1)

    @pl.when((b == 0) & (i == 0))
    def _():
        dif = (lax.broadcasted_iota(I32, (tb, tb), 0) - lax.broadcasted_iota(I32, (tb, tb), 1)).astype(F32)
        rowf = lax.broadcasted_iota(I32, (tb, RET_DK), 0).astype(F32)
        for h in range(RET_HEADS):
            lg = _RET_LOG_GAMMA[h]
            dmat[h] = jnp.where(dif >= 0.0, jnp.exp(jnp.maximum(dif, 0.0) * lg), 0.0)
            gq[h] = jnp.exp((rowf + 1.0) * lg)
            we[h] = jnp.exp((tb - 1.0 - rowf) * lg)

    @pl.when(i == 0)
    def _():
        s_ref[...] = s0_ref[...]

    cos2 = cs_ref[0]
    sin2 = cs_ref[1]
    for s in range(sb):
        rows = slice(s * tb, (s + 1) * tb)
        outs = []
        for h in range(RET_HEADS):
            c0 = h * RET_DK
            qh = qk_ref[rows, c0:c0 + RET_DK]
            kh = qk_ref[rows, BRANCH_W + c0:BRANCH_W + c0 + RET_DK]
            vh = vg_ref[rows, c0:c0 + RET_DK]
            gh = vg_ref[rows, BRANCH_W + c0:BRANCH_W + c0 + RET_DK]
            qr = qh * cos2 + pltpu.roll(qh, RET_DK // 2, axis=1) * sin2
            kr = (kh * cos2 + pltpu.roll(kh, RET_DK // 2, axis=1) * sin2) * (RET_DK ** -0.5)
            st = s_ref[s, h]
            att = _dot_nt(qr, kr) * dmat[h]
            o = _dot(att, vh) + _dot(qr * gq[h], st)
            s_ref[s, h] = math.exp(tb * _RET_LOG_GAMMA[h]) * st + _dot_tn(kr * we[h], vh)
            mu = jnp.mean(o, axis=-1, keepdims=True)
            oc = o - mu
            var = jnp.mean(oc * oc, axis=-1, keepdims=True)
            on = oc * lax.rsqrt(var + EPS) * ng_ref[:, c0:c0 + RET_DK]
            outs.append(_silu(gh) * on)
        out_ref[rows, :] = jnp.concatenate(outs, axis=1).astype(out_ref.dtype)


def _ret_mixer(proj, cs_tab, s0, ng, *, bn, seq, sb, tb):
    nt = seq // tb
    rows = sb * tb
    kern = functools.partial(_ret_kernel, sb=sb, tb=tb)
    st_spec = pl.BlockSpec((sb, RET_HEADS, RET_DK, RET_DK), lambda b, i: (b, 0, 0, 0))
    cb = COL_RET // (2 * BRANCH_W)
    return pl.pallas_call(
        kern,
        out_shape=(jax.ShapeDtypeStruct((bn * seq, BRANCH_W), BF16),
                   jax.ShapeDtypeStruct((bn, RET_HEADS, RET_DK, RET_DK), F32)),
        grid=(bn // sb, nt),
        in_specs=[pl.BlockSpec((rows, 2 * BRANCH_W), lambda b, i: (b * nt + i, cb)),
                  pl.BlockSpec((rows, 2 * BRANCH_W), lambda b, i: (b * nt + i, cb + 1)),
                  pl.BlockSpec((2, tb, RET_DK), lambda b, i: (0, i, 0)),
                  st_spec, _full_spec(ng)],
        out_specs=(pl.BlockSpec((rows, BRANCH_W), lambda b, i: (b * nt + i, 0)), st_spec),
        scratch_shapes=[pltpu.VMEM((RET_HEADS, tb, tb), F32),
                        pltpu.VMEM((RET_HEADS, tb, RET_DK), F32),
                        pltpu.VMEM((RET_HEADS, tb, RET_DK), F32)],
        compiler_params=_cparams(("arbitrary", "arbitrary")),
        name="ret_mixer",
    )(proj, proj, cs_tab, s0, ng)


def _rotary_tables(pos0, seq):
    half = RET_DK // 2
    pos = pos0 + jnp.arange(seq, dtype=F32)
    inv = 1.0 / (ROPE_BASE ** jnp.linspace(0.0, 1.0, half, dtype=F32))
    ang = pos[:, None] * inv
    cos, sin = jnp.cos(ang), jnp.sin(ang)
    return jnp.stack([jnp.concatenate([cos, cos], axis=1), jnp.concatenate([-sin, sin], axis=1)])


def _block_ones():
    blk = np.kron(np.eye(HG_HEADS, dtype=np.float32), np.ones((HG_DK, HG_DK), np.float32))
    return jnp.asarray(blk, BF16)


def _hg_lower(hgl_ref, layer):
    raw = hgl_ref[...]
    e = jnp.exp(raw - jnp.max(raw, axis=0, keepdims=True))
    p = e / jnp.sum(e, axis=0, keepdims=True)
    cum = p[0:1]
    for l in range(1, layer + 1):
        cum = cum + p[l:l + 1]
    return cum - p[0:1]


HG_FACTOR_ROWS = 32
HG_SAFE_SPAN = 60.0


def _hg_intra_bounded(q, kk, vi, cum, ones_ref, c, w):
    rmod = _row_iota((c, BRANCH_W)) % w
    o = _dot((q * kk).astype(BF16), ones_ref[...]) * vi
    for delta in range(1, w):
        dec = jnp.exp(jnp.minimum(cum - pltpu.roll(cum, delta, axis=0), 0.0))
        p = q * dec * pltpu.roll(kk, delta, axis=0)
        rs = _dot(p.astype(BF16), ones_ref[...])
        o = o + jnp.where(rmod >= delta, rs, 0.0) * pltpu.roll(vi, delta, axis=0)
    nsub = c // w
    if nsub > 1:
        off = [jnp.zeros((w, BRANCH_W), F32)]
        for i in range(1, nsub):
            ref_row = cum[i * w - 1:i * w, :]
            qt = q[i * w:(i + 1) * w] * jnp.exp(cum[i * w:(i + 1) * w] - ref_row)
            kt = kk[0:i * w] * jnp.exp(ref_row - cum[0:i * w])
            parts = []
            for h in range(HG_HEADS):
                hs = slice(h * HG_DK, (h + 1) * HG_DK)
                att = _dot_nt(qt[:, hs], kt[:, hs])
                parts.append(_dot(att, vi[0:i * w, hs]))
            off.append(jnp.concatenate(parts, axis=1))
        o = o + jnp.concatenate(off, axis=0)
    return o


def _hg_intra_factored(q, kk, vi, cum, c, w):
    head_of_lane = lax.broadcasted_iota(I32, (w, BRANCH_W), 1) // HG_DK
    outs = []
    for i in range(c // w):
        lo, hi = i * w, (i + 1) * w
        base = cum[lo - 1:lo, :] if i > 0 else jnp.zeros((1, BRANCH_W), F32)
        qt = q[lo:hi] * jnp.exp(cum[lo:hi] - base)
        kt = kk[0:hi] * jnp.exp(base - cum[0:hi])
        q_bd = jnp.concatenate([jnp.where(head_of_lane == h, qt, 0.0) for h in range(HG_HEADS)], axis=0)
        rows = _row_iota((HG_HEADS * w, hi)) % w + lo
        causal = rows >= lax.broadcasted_iota(I32, (HG_HEADS * w, hi), 1)
        att = jnp.where(causal, _dot_nt(q_bd, kt), 0.0)
        full = _dot(att, vi[0:hi, :])
        outs.append(jnp.concatenate(
            [full[h * w:(h + 1) * w, h * HG_DK:(h + 1) * HG_DK] for h in range(HG_HEADS)], axis=1))
    return outs[0] if len(outs) == 1 else jnp.concatenate(outs, axis=0)


def _hg_chunk(q, f, vi, st_refs, o_scr, log_lb, log_1mlb, ones_ref, c, w):
    ls = jnp.minimum(f, 0.0) - jnp.log(1.0 + jnp.exp(-jnp.abs(f)))
    b = log_1mlb + ls
    lf = jnp.maximum(log_lb, b) + jnp.log(1.0 + jnp.exp(-jnp.abs(log_lb - b)))
    kk = _one_minus_exp(lf)
    cum = _cumsum_rows(lf) if c > 1 else lf
    wf = min(HG_FACTOR_ROWS, c)
    span = -cum[wf - 1:wf, :]
    for i in range(1, c // wf):
        span = jnp.maximum(span, cum[i * wf - 1:i * wf, :] - cum[(i + 1) * wf - 1:(i + 1) * wf, :])
    small = jnp.max(span) <= HG_SAFE_SPAN

    @pl.when(small)
    def _():
        o_scr[...] = _hg_intra_factored(q, kk, vi, cum, c, wf)

    @pl.when(jnp.logical_not(small))
    def _():
        o_scr[...] = _hg_intra_bounded(q, kk, vi, cum, ones_ref, c, w)

    o = o_scr[...]
    qg = q * jnp.exp(cum)
    cl = cum[c - 1:c, :]
    kw = kk * jnp.exp(cl - cum)
    dec_end = jnp.exp(cl)
    inter = []
    for h in range(HG_HEADS):
        hs = slice(h * HG_DK, (h + 1) * HG_DK)
        st = st_refs(h)
        inter.append(_dot_nt(qg[:, hs], st[...]))
        st[...] = st[...] * dec_end[:, hs] + _dot_tn(vi[:, hs], kw[:, hs])
    return o + jnp.concatenate(inter, axis=1)


def _hg_kernel(qf_ref, ig_ref, hgl_ref, ones_ref, s0_ref, ng_ref, out_ref, s_ref, o_scr, *, sb, tb, c, w, layer, nt):
    i = pl.program_id(1)

    @pl.when(i == 0)
    def _():
        for s in range(sb):
            for h in range(HG_HEADS):
                s_ref[s, h] = s0_ref[s, h].T

    lower = _hg_lower(hgl_ref, layer)
    log_lb = jnp.log(jnp.maximum(lower, LB_FLOOR))
    log_1mlb = jnp.log(1.0 - lower)
    for s in range(sb):
        def chunk(ci, carry):
            r0 = pl.multiple_of(s * tb + ci * c, V7X_SUBLANES)
            rows = pl.ds(r0, c)
            q = qf_ref[rows, 0:BRANCH_W]
            f = qf_ref[rows, BRANCH_W:2 * BRANCH_W]
            vi = ig_ref[rows, 0:BRANCH_W]
            g = ig_ref[rows, BRANCH_W:2 * BRANCH_W]
            o = _hg_chunk(q, f, vi, lambda h: s_ref.at[s, h], o_scr, log_lb, log_1mlb, ones_ref, c, w)
            outs = []
            for h in range(HG_HEADS):
                hs = slice(h * HG_DK, (h + 1) * HG_DK)
                oh = o[:, hs]
                ms = jnp.mean(oh * oh, axis=-1, keepdims=True)
                outs.append(oh * lax.rsqrt(ms + EPS))
            on = jnp.concatenate(outs, axis=1) * ng_ref[...]
            out_ref[rows, :] = (_silu(g) * on).astype(out_ref.dtype)
            return carry

        if tb == c:
            chunk(0, 0)
        else:
            lax.fori_loop(0, tb // c, chunk, 0)

    @pl.when(i == nt - 1)
    def _():
        for s in range(sb):
            for h in range(HG_HEADS):
                s_ref[s, h] = s_ref[s, h].T


def _hg_mixer(proj, hgl, ones_bd, s0, ng, *, bn, seq, sb, tb, layer):
    nt = seq // tb
    rows = sb * tb
    c = min(HG_CHUNK, tb)
    w = min(HG_SUB, c)
    kern = functools.partial(_hg_kernel, sb=sb, tb=tb, c=c, w=w, layer=layer, nt=nt)
    st_spec = pl.BlockSpec((sb, HG_HEADS, HG_DK, HG_DK), lambda b, i: (b, 0, 0, 0))
    cb = COL_HG // (2 * BRANCH_W)
    return pl.pallas_call(
        kern,
        out_shape=(jax.ShapeDtypeStruct((bn * seq, BRANCH_W), BF16),
                   jax.ShapeDtypeStruct((bn, HG_HEADS, HG_DK, HG_DK), F32)),
        grid=(bn // sb, nt),
        in_specs=[pl.BlockSpec((rows, 2 * BRANCH_W), lambda b, i: (b * nt + i, cb)),
                  pl.BlockSpec((rows, 2 * BRANCH_W), lambda b, i: (b * nt + i, cb + 1)),
                  _full_spec(hgl), _full_spec(ones_bd), st_spec, _full_spec(ng)],
        out_specs=(pl.BlockSpec((rows, BRANCH_W), lambda b, i: (b * nt + i, 0)), st_spec),
        scratch_shapes=[pltpu.VMEM((c, BRANCH_W), F32)],
        compiler_params=_cparams(("arbitrary", "arbitrary")),
        name="hg_mixer",
    )(proj, proj, hgl, ones_bd, s0, ng)


SSM_GW = SSM_GROUPS * SSM_STATE
SSM_HPG = SSM_HEADS // SSM_GROUPS


def _ssd_prep(proj, cw, cb, dt_bias, a_log, dvec, ng):
    dt_t = proj[:, COL_DT:COL_DT + SSM_HEADS].T
    pad = DT_PAD - SSM_HEADS
    dtb = jnp.pad(dt_bias, (0, pad))[None, :]
    alog = jnp.pad(a_log, (0, pad))[None, :]
    return (dt_t, cw, cb[None, :], dtb, alog, dt_bias[:, None], a_log[:, None],
            jnp.repeat(dvec, SSM_HEADDIM)[None, :], ng[None, :])


def _seg_cumsum(x, c, axis):
    pos = lax.broadcasted_iota(I32, x.shape, axis) % c
    d = 1
    while d < c:
        x = x + jnp.where(pos >= d, pltpu.roll(x, d, axis=axis), 0.0)
        d *= 2
    return x


def _ssd_kernel(xbc_ref, z_ref, dt_ref, dtt_ref, prev_ref, s0_ref, cw_ref, cb_ref, dtb_ref, alog_ref, dtbc_ref,
                alogc_ref, dvec_ref, ng_ref, out_ref, tail_ref, s_ref, *, sb, tb, c):
    i = pl.program_id(1)

    @pl.when(i == 0)
    def _():
        tail_ref[...] = prev_ref[...]
        s_ref[...] = s0_ref[...]

    dt_all = _softplus(dt_ref[...] + dtb_ref[...])
    cum_all = _seg_cumsum(-dt_all * jnp.exp(alog_ref[...]), c, 0)
    dtt_all = _softplus(dtt_ref[...] + dtbc_ref[...])
    cumt_all = _seg_cumsum(-dtt_all * jnp.exp(alogc_ref[...]), c, 1)
    causal = lax.broadcasted_iota(I32, (c, c), 0) >= lax.broadcasted_iota(I32, (c, c), 1)
    for s in range(sb):
        rows = slice(s * tb, (s + 1) * tb)
        xin = xbc_ref[rows, :]
        act = _silu(_conv_taps(xin, tail_ref[s], cw_ref, cb_ref))
        tail_ref[s] = xin[tb - V7X_SUBLANES:tb]
        zs = z_ref[rows, :]
        for ci in range(tb // c):
            r0 = s * tb + ci * c
            cr = slice(ci * c, (ci + 1) * c)
            xs = act[cr, 0:BRANCH_W]
            bm = act[cr, BRANCH_W:BRANCH_W + SSM_GW]
            cm = act[cr, BRANCH_W + SSM_GW:BRANCH_W + 2 * SSM_GW]
            cum = cum_all[r0:r0 + c, :]
            dt = dt_all[r0:r0 + c, :]
            cumt = cumt_all[:, r0:r0 + c]
            dtt = dtt_all[:, r0:r0 + c]
            ys = []
            for g in range(SSM_GROUPS):
                gs = slice(g * SSM_STATE, (g + 1) * SSM_STATE)
                cmg, bmg = cm[:, gs], bm[:, gs]
                cb_g = _dot_nt(cmg, bmg)
                for hh in range(SSM_HPG):
                    h = g * SSM_HPG + hh
                    xh = xs[:, h * SSM_HEADDIM:(h + 1) * SSM_HEADDIM]
                    ccol = cum[:, h:h + 1]
                    seg = ccol - cumt[h:h + 1, :]
                    dec = jnp.where(causal, jnp.exp(jnp.where(causal, seg, 0.0)), 0.0)
                    att = cb_g * dec * dtt[h:h + 1, :]
                    st = s_ref[s, h]
                    y = _dot(att, xh) + _dot(cmg * jnp.exp(ccol), st)
                    cl = cum[c - 1:c, h:h + 1]
                    wcol = dt[:, h:h + 1] * jnp.exp(cl - ccol)
                    s_ref[s, h] = jnp.exp(cl) * st + _dot_tn(bmg * wcol, xh)
                    ys.append(y)
            y = jnp.concatenate(ys, axis=1) + dvec_ref[...] * xs
            y = y * _silu(zs[cr, :])
            gw = BRANCH_W // SSM_GROUPS
            outs = []
            for g in range(SSM_GROUPS):
                yg = y[:, g * gw:(g + 1) * gw]
                ms = jnp.mean(yg * yg, axis=-1, keepdims=True)
                outs.append(yg * lax.rsqrt(ms + EPS))
            yn = jnp.concatenate(outs, axis=1) * ng_ref[...]
            out_ref[r0:r0 + c, :] = yn.astype(out_ref.dtype)


def _ssd_mixer(proj, dt_t, prev8, s0, cw, cb, dtb, alog, dtbc, alogc, dvec, ng, *, bn, seq, sb, tb):
    nt = seq // tb
    rows = sb * tb
    c = min(HG_CHUNK, tb)
    kern = functools.partial(_ssd_kernel, sb=sb, tb=tb, c=c)
    st_spec = pl.BlockSpec((sb, SSM_HEADS, SSM_STATE, SSM_HEADDIM), lambda b, i: (b, 0, 0, 0))
    tl_spec = pl.BlockSpec((sb, V7X_SUBLANES, SSM_XBC), lambda b, i: (b, 0, 0))
    return pl.pallas_call(
        kern,
        out_shape=(jax.ShapeDtypeStruct((bn * seq, BRANCH_W), BF16),
                   jax.ShapeDtypeStruct((bn, V7X_SUBLANES, SSM_XBC), F32),
                   jax.ShapeDtypeStruct((bn, SSM_HEADS, SSM_STATE, SSM_HEADDIM), F32)),
        grid=(bn // sb, nt),
        in_specs=[pl.BlockSpec((rows, SSM_XBC), lambda b, i: (b * nt + i, COL_XBC // SSM_XBC)),
                  pl.BlockSpec((rows, BRANCH_W), lambda b, i: (b * nt + i, COL_Z // BRANCH_W)),
                  pl.BlockSpec((rows, DT_PAD), lambda b, i: (b * nt + i, COL_DT // DT_PAD)),
                  pl.BlockSpec((SSM_HEADS, rows), lambda b, i: (0, b * nt + i)),
                  tl_spec, st_spec,
                  _full_spec(cw), _full_spec(cb), _full_spec(dtb), _full_spec(alog), _full_spec(dtbc),
                  _full_spec(alogc), _full_spec(dvec), _full_spec(ng)],
        out_specs=(pl.BlockSpec((rows, BRANCH_W), lambda b, i: (b * nt + i, 0)), tl_spec, st_spec),
        compiler_params=_cparams(("arbitrary", "arbitrary")),
        name="ssd_mixer",
    )(proj, proj, proj, dt_t, prev8, s0, cw, cb, dtb, alog, dtbc, alogc, dvec, ng)


def _merge_kernel(x_ref, b0_ref, b1_ref, b2_ref, b3_ref, g0_ref, g1_ref, g2_ref, g3_ref, wb_ref, wo_ref,
                  lg_ref, lb_ref, y_ref, yb_ref):
    acc = None
    for br, gt, k in ((b0_ref, g0_ref, 0), (b1_ref, g1_ref, 1), (b2_ref, g2_ref, 2), (b3_ref, g3_ref, 3)):
        term = _sigmoid(gt[...]) * _dot(br[...], wb_ref[k])
        acc = term if acc is None else acc + term
    mix = _dot(acc.astype(BF16), wo_ref[...])
    y = _layer_norm_rows(ALPHA * x_ref[...] + mix, lg_ref[...], lb_ref[...])
    y_ref[...] = y
    yb_ref[...] = y.astype(BF16)


def _merge(x, brs, proj, wb, wo, lg, lb):
    t = x.shape[0]
    tm = min(512, t)
    row = lambda i: (i, 0)
    gate_specs = [pl.BlockSpec((tm, D_MODEL), functools.partial(lambda i, k: (i, COL_GATE // D_MODEL + k), k=k))
                  for k in range(N_BRANCH)]
    return pl.pallas_call(
        _merge_kernel,
        out_shape=(jax.ShapeDtypeStruct((t, D_MODEL), F32), jax.ShapeDtypeStruct((t, D_MODEL), BF16)),
        grid=(t // tm,),
        in_specs=[pl.BlockSpec((tm, D_MODEL), row)] + [pl.BlockSpec((tm, BRANCH_W), row)] * N_BRANCH + gate_specs
                 + [_full_spec(wb), _full_spec(wo), _full_spec(lg), _full_spec(lb)],
        out_specs=(pl.BlockSpec((tm, D_MODEL), row), pl.BlockSpec((tm, D_MODEL), row)),
        compiler_params=_cparams(("arbitrary",)),
        name="merge_ln",
    )(x, *brs, proj, proj, proj, proj, wb, wo, lg, lb)


PEER_TR = 128
PEER_SLOTS = PEER_HEADS * PEER_TOPK
_CAND_GROUPS = ((0, 16), (1, 8), (2, 5), (3, 4), (4, 3), (5, 2), (6, 2), (7, 2))
PEER_NCAND = 16 + 8 * 7 + 8


def _cand_tables():
    ids = np.zeros((PEER_NCAND,), np.int32)
    msk = np.zeros((PEER_NCAND,), np.float32)
    r = 0
    for a, nb in _CAND_GROUPS:
        width = 16 if a == 0 else 8
        for b in range(width):
            ids[r] = a * PEER_TOPK + b
            msk[r] = 0.0 if b < nb else NEG_INF
            r += 1
    for a in range(8, 16):
        ids[r] = a * PEER_TOPK
        r += 1
    assert r == PEER_NCAND
    return (jnp.asarray(np.repeat(ids[:, None], PEER_TR, axis=1)),
            jnp.asarray(np.repeat(msk[:, None], PEER_TR, axis=1)))


def _top16(s, ids, big):
    vals, idxs = [], []
    for _ in range(PEER_TOPK):
        m = jnp.max(s, axis=0, keepdims=True)
        idx = jnp.min(jnp.where(s == m, ids, big), axis=0, keepdims=True)
        vals.append(m)
        idxs.append(idx)
        s = jnp.where(ids == idx, NEG_INF, s)
    return jnp.concatenate(vals, axis=0), jnp.concatenate(idxs, axis=0)


def _route_kernel(x_ref, wq_ref, keys_ref, cid_ref, cmask_ref, i1_ref, i2_ref, g_ref, q_scr, v_scr, n_scr):
    q_scr[...] = _dot(x_ref[...], wq_ref[...])
    key_ids = _row_iota((N_KEYS, PEER_TR))

    def stage1(j, carry):
        c0 = pl.multiple_of(j * PEER_HALF, PEER_HALF)
        qh = q_scr[:, pl.ds(c0, PEER_HALF)]
        st = _dot_nt(keys_ref[j % 2], qh)
        vals, idxs = _top16(st, key_ids, N_KEYS)
        v_scr[j] = vals
        n_scr[j] = idxs
        return carry

    lax.fori_loop(0, 2 * PEER_HEADS, stage1, 0, unroll=4)

    def stage2(h, carry):
        v1, v2 = v_scr[2 * h], v_scr[2 * h + 1]
        n1, n2 = n_scr[2 * h], n_scr[2 * h + 1]
        rows = [v1[0:1] + v2]
        for a, _ in _CAND_GROUPS[1:]:
            rows.append(v1[a:a + 1] + v2[0:8])
        rows.append(v1[8:16] + v2[0:1])
        cand = jnp.concatenate(rows, axis=0) + cmask_ref[...]
        sc, pos = _top16(cand, cid_ref[...], PEER_TOPK * PEER_TOPK)
        ra = lax.shift_right_logical(pos, 4)
        rb = lax.bitwise_and(pos, PEER_TOPK - 1)
        e1 = jnp.zeros((PEER_TOPK, PEER_TR), I32)
        e2 = jnp.zeros((PEER_TOPK, PEER_TR), I32)
        for r in range(PEER_TOPK):
            e1 = jnp.where(ra == r, n1[r:r + 1], e1)
            e2 = jnp.where(rb == r, n2[r:r + 1], e2)
        ex = jnp.exp(sc - sc[0:1])
        gw = ex / jnp.sum(ex, axis=0, keepdims=True)
        i1_ref[0, pl.ds(pl.multiple_of(h * PEER_TOPK, PEER_TOPK), PEER_TOPK), :] = e1
        i2_ref[0, pl.ds(pl.multiple_of(h * PEER_TOPK, PEER_TOPK), PEER_TOPK), :] = e2
        g_ref[0, pl.ds(pl.multiple_of(h * PEER_TOPK, PEER_TOPK), PEER_TOPK), :] = gw
        return carry

    lax.fori_loop(0, PEER_HEADS, stage2, 0, unroll=4)


def _peer_route(xb, wq, keys, cid, cmask):
    t = xb.shape[0]
    nb = t // PEER_TR
    slot_spec = pl.BlockSpec((1, PEER_SLOTS, PEER_TR), lambda i: (i, 0, 0))
    i1, i2, g = pl.pallas_call(
        _route_kernel,
        out_shape=(jax.ShapeDtypeStruct((nb, PEER_SLOTS, PEER_TR), I32),
                   jax.ShapeDtypeStruct((nb, PEER_SLOTS, PEER_TR), I32),
                   jax.ShapeDtypeStruct((nb, PEER_SLOTS, PEER_TR), F32)),
        grid=(nb,),
        in_specs=[pl.BlockSpec((PEER_TR, D_MODEL), lambda i: (i, 0)),
                  _full_spec(wq), _full_spec(keys), _full_spec(cid), _full_spec(cmask)],
        out_specs=(slot_spec, slot_spec, slot_spec),
        scratch_shapes=[pltpu.VMEM((PEER_TR, PEER_HEADS * PEER_QDIM), F32),
                        pltpu.VMEM((2 * PEER_HEADS, PEER_TOPK, PEER_TR), F32),
                        pltpu.VMEM((2 * PEER_HEADS, PEER_TOPK, PEER_TR), I32)],
        compiler_params=_cparams(("arbitrary",)),
        name="peer_route",
    )(xb, wq, keys, cid, cmask)
    tok = lambda a: jnp.swapaxes(a, 1, 2).reshape(t, PEER_SLOTS)
    return tok(i1), tok(i2), tok(g)


PEER_TB = 512
PEER_ET = 1024
PEER_EC = 512
C_HALF = N_KEYS // 2
C_PITCH = C_HALF + 8
U32 = jnp.uint32
HI16 = 0xFFFF0000


def _experts_kernel(xb_ref, x_ref, i1_ref, i2_ref, g_ref, ut_ref, v_ref, lg_ref, lb_ref, y_ref, yb_ref,
                    c_scr, acc_scr, *, tb, et, ne):
    e = pl.program_id(1)

    @pl.when(e == 0)
    def _():
        sub = _row_iota((N_KEYS, PEER_SLOTS))

        def build(t, carry):
            row = pl.ds(t, 1)
            pt = jnp.where(sub == i1_ref[row, :], g_ref[row, :], 0.0).astype(BF16)
            qt = jnp.where(sub == i2_ref[row, :], 1.0, 0.0).astype(BF16)
            c = _dot_nt(pt, qt).astype(BF16).astype(F32)
            hi = lax.bitcast_convert_type(c[0:C_HALF], U32)
            lo = lax.shift_right_logical(lax.bitcast_convert_type(c[C_HALF:N_KEYS], U32), jnp.uint32(16))
            c_scr[pl.ds(pl.multiple_of(t * C_PITCH, V7X_SUBLANES), C_HALF), :] = hi | lo
            return carry

        lax.fori_loop(0, tb, build, 0, unroll=16)

    nk = et // N_KEYS
    steps_per_half = C_HALF // nk
    row0 = (e % steps_per_half) * nk
    shift = ((e // steps_per_half) * 16).astype(U32)
    per_chunk = PEER_EC // N_KEYS
    part = None
    for c in range(et // PEER_EC):
        cols = slice(c * PEER_EC, (c + 1) * PEER_EC)
        act = _gelu(_dot(xb_ref[...], ut_ref[:, cols]))
        words = jnp.concatenate(
            [c_scr[pl.ds(row0 + c * per_chunk + j, tb, stride=C_PITCH), :] for j in range(per_chunk)], axis=1)
        coef = lax.bitcast_convert_type(jnp.left_shift(words, shift) & jnp.uint32(HI16), F32)
        term = _dot((act * coef).astype(BF16), v_ref[cols, :])
        part = term if part is None else part + term

    @pl.when(e == 0)
    def _():
        acc_scr[...] = part

    @pl.when(e > 0)
    def _():
        acc_scr[...] += part

    @pl.when(e == ne - 1)
    def _():
        y = _layer_norm_rows(ALPHA * x_ref[...] + acc_scr[...], lg_ref[...], lb_ref[...])
        y_ref[...] = y
        yb_ref[...] = y.astype(BF16)


def _peer_experts(xb, x, i1, i2, g, ut, v, lg, lb):
    t = x.shape[0]
    tb = min(PEER_TB, t)
    et = PEER_ET
    ne = N_EXPERTS // et
    kern = functools.partial(_experts_kernel, tb=tb, et=et, ne=ne)
    row = lambda i, e: (i, 0)
    return pl.pallas_call(
        kern,
        out_shape=(jax.ShapeDtypeStruct((t, D_MODEL), F32), jax.ShapeDtypeStruct((t, D_MODEL), BF16)),
        grid=(t // tb, ne),
        in_specs=[pl.BlockSpec((tb, D_MODEL), row), pl.BlockSpec((tb, D_MODEL), row),
                  pl.BlockSpec((tb, PEER_SLOTS), row), pl.BlockSpec((tb, PEER_SLOTS), row),
                  pl.BlockSpec((tb, PEER_SLOTS), row),
                  pl.BlockSpec((D_MODEL, et), lambda i, e: (0, e)),
                  pl.BlockSpec((et, D_MODEL), lambda i, e: (e, 0)),
                  _full_spec(lg), _full_spec(lb)],
        out_specs=(pl.BlockSpec((tb, D_MODEL), row), pl.BlockSpec((tb, D_MODEL), row)),
        scratch_shapes=[pltpu.VMEM((tb * C_PITCH, N_KEYS), U32), pltpu.VMEM((tb, D_MODEL), F32)],
        compiler_params=_cparams(("arbitrary", "arbitrary")),
        name="peer_experts",
    )(xb, x, i1, i2, g, ut, v, lg, lb)


def _prep_layer(l, w_in, rg_conv_w, rg_conv_b, rg_wa, rg_ba, rg_wx, rg_bx, rg_lambda, ret_norm_g, hg_norm_g,
                ssm_conv_w, ssm_conv_b, ssm_dt_bias, ssm_a_log, ssm_d, ssm_norm_g, w_branch, w_out, ln1_g, ln1_b,
                peer_wq, peer_keys, peer_u, peer_v, ln2_g, ln2_b):
    w = w_in[l]
    z0 = COL_HG + 2048
    x0 = z0 + BRANCH_W
    d0 = x0 + SSM_XBC
    g0 = d0 + SSM_HEADS
    w_k = jnp.concatenate([w[:, :z0], w[:, x0:d0], w[:, g0:], w[:, z0:x0], w[:, d0:g0],
                           jnp.zeros((D_MODEL, DT_PAD - SSM_HEADS), w.dtype)], axis=1).astype(BF16)
    eye = jnp.eye(RG_BLOCKS, dtype=F32)
    bd = lambda m: jnp.einsum('kij,kl->kilj', m, eye).reshape(BRANCH_W, BRANCH_W)
    return dict(
        w_in=w_k,
        rg=(rg_conv_w[l], rg_conv_b[l][None, :],
            jnp.concatenate([bd(rg_wa[l]), bd(rg_wx[l])], axis=1).astype(BF16),
            jnp.concatenate([rg_ba[l], rg_bx[l]])[None, :], rg_lambda[l][None, :]),
        ret_ng=ret_norm_g[l][None, :],
        hg_ng=hg_norm_g[l][None, :],
        ssd=(ssm_conv_w[l], ssm_conv_b[l], ssm_dt_bias[l], ssm_a_log[l], ssm_d[l], ssm_norm_g[l]),
        wb=w_branch[l].astype(BF16), wo=w_out[l].astype(BF16),
        ln1=(ln1_g[l][None, :], ln1_b[l][None, :]),
        wq=peer_wq[l].astype(BF16), keys=peer_keys[l],
        ut=peer_u[l].T.astype(BF16), v=peer_v[l].astype(BF16),
        ln2=(ln2_g[l][None, :], ln2_b[l][None, :]),
    )


def _pad_hist(buf):
    return jnp.pad(buf, ((0, 0), (V7X_SUBLANES - (CONV_W - 1), 0), (0, 0)))


def _layer(l, x, xb, state, lp, hg_lower, ones_bd, cand, cs_tab, *, bn, seq, sb, tb):
    rg_h, rg_buf, ret_s, hg_s, ssm_s, ssm_buf = state
    geo = dict(bn=bn, seq=seq, sb=sb, tb=tb)
    proj = _in_proj(xb, lp['w_in'])
    rg_out, rg_tail, rg_hl = _rg_mixer(proj, _pad_hist(rg_buf), rg_h[:, None, :], *lp['rg'], **geo)
    ret_out, ret_new = _ret_mixer(proj, cs_tab, ret_s, lp['ret_ng'], **geo)
    hg_out, hg_new = _hg_mixer(proj, hg_lower, ones_bd, hg_s, lp['hg_ng'], layer=l, **geo)
    dt_t, cw, cb, dtb, alog, dtbc, alogc, dvec, ng = _ssd_prep(proj, *lp['ssd'])
    ssd_out, ssd_tail, ssm_new = _ssd_mixer(proj, dt_t, _pad_hist(ssm_buf), ssm_s, cw, cb, dtb, alog, dtbc, alogc,
                                            dvec, ng, **geo)
    x1, x1b = _merge(x, (rg_out, ret_out, hg_out, ssd_out), proj, lp['wb'], lp['wo'], *lp['ln1'])
    i1, i2, g = _peer_route(x1b, lp['wq'], lp['keys'], *cand)
    x2, x2b = _peer_experts(x1b, x1, i1, i2, g, lp['ut'], lp['v'], *lp['ln2'])
    hist = slice(V7X_SUBLANES - (CONV_W - 1), V7X_SUBLANES)
    new_state = (rg_hl[:, V7X_SUBLANES - 1], rg_tail[:, hist], ret_new, hg_new, ssm_new, ssd_tail[:, hist])
    return x2, x2b, new_state


def kernel(x_prompt, x_sample, state_rglru_h, state_rglru_conv, state_ret, state_hgrn, state_ssm, state_ssm_conv, w_in, rg_conv_w, rg_conv_b, rg_wa, rg_ba, rg_wx, rg_bx, rg_lambda, ret_norm_g, hg_lower, hg_norm_g, ssm_conv_w, ssm_conv_b, ssm_dt_bias, ssm_a_log, ssm_d, ssm_norm_g, w_branch, w_out, ln1_g, ln1_b, peer_wq, peer_keys, peer_u, peer_v, ln2_g, ln2_b):
    bp, lp_len, _ = x_prompt.shape
    bs, ls_len, _ = x_sample.shape
    sdt = state_ret.dtype
    ones_bd = _block_ones()
    cand = _cand_tables()
    cs_p = _rotary_tables(0.0, lp_len)
    cs_s = _rotary_tables(float(PAST_LEN), ls_len)
    geo_p = dict(bn=bp, seq=lp_len, sb=1, tb=256)
    geo_s = dict(bn=bs, seq=ls_len, sb=16, tb=ls_len)

    xp = x_prompt.reshape(bp * lp_len, D_MODEL)
    xs = x_sample.reshape(bs * ls_len, D_MODEL)
    xpb, xsb = xp.astype(BF16), xs.astype(BF16)
    new_p, new_s = [], []
    for l in range(DEPTH):
        lp = _prep_layer(l, w_in, rg_conv_w, rg_conv_b, rg_wa, rg_ba, rg_wx, rg_bx, rg_lambda, ret_norm_g, hg_norm_g,
                         ssm_conv_w, ssm_conv_b, ssm_dt_bias, ssm_a_log, ssm_d, ssm_norm_g, w_branch, w_out, ln1_g,
                         ln1_b, peer_wq, peer_keys, peer_u, peer_v, ln2_g, ln2_b)
        zero = (jnp.zeros((bp, BRANCH_W), sdt), jnp.zeros((bp, CONV_W - 1, BRANCH_W), sdt),
                jnp.zeros((bp, RET_HEADS, RET_DK, RET_DK), sdt), jnp.zeros((bp, HG_HEADS, HG_DK, HG_DK), sdt),
                jnp.zeros((bp, SSM_HEADS, SSM_STATE, SSM_HEADDIM), sdt), jnp.zeros((bp, CONV_W - 1, SSM_XBC), sdt))
        xp, xpb, sp = _layer(l, xp, xpb, zero, lp, hg_lower, ones_bd, cand, cs_p, **geo_p)
        carried = (state_rglru_h[l], state_rglru_conv[l], state_ret[l], state_hgrn[l], state_ssm[l], state_ssm_conv[l])
        xs, xsb, ss = _layer(l, xs, xsb, carried, lp, hg_lower, ones_bd, cand, cs_s, **geo_s)
        new_p.append(sp)
        new_s.append(ss)
    p_states = [jnp.stack(t) for t in zip(*new_p)]
    s_states = [jnp.stack(t) for t in zip(*new_s)]
    return (xp.reshape(bp, lp_len, D_MODEL), xs.reshape(bs, ls_len, D_MODEL), *p_states, *s_states)
```

```python
import functools
import math

import numpy as np
import jax
import jax.numpy as jnp
from jax import lax
from jax.experimental import pallas as pl
from jax.experimental.pallas import tpu as pltpu

F32 = jnp.float32
BF16 = jnp.bfloat16
I32 = jnp.int32

D_MODEL = 1024
DEPTH = 2
PAST_LEN = 16384
EPS = 1e-5
LB_FLOOR = 1e-20
BRANCH_W = D_MODEL // 2
N_BRANCH = 4
CONV_W = 4
RG_BLOCKS = 8
RG_BLOCK = BRANCH_W // RG_BLOCKS
RG_C = 8.0
RET_HEADS = 4
RET_DK = BRANCH_W // RET_HEADS
ROPE_BASE = 10000.0
HG_HEADS = 4
HG_DK = BRANCH_W // HG_HEADS
SSM_HEADDIM = 64
SSM_HEADS = BRANCH_W // SSM_HEADDIM
SSM_GROUPS = 2
SSM_STATE = 128
SSM_XBC = BRANCH_W + 2 * SSM_GROUPS * SSM_STATE
PEER_HEADS = 8
N_KEYS = 128
N_EXPERTS = N_KEYS * N_KEYS
PEER_TOPK = 16
PEER_QDIM = 256
PEER_HALF = PEER_QDIM // 2
ALPHA = (2.0 * DEPTH) ** 0.25

V7X_LANES = 128
V7X_SUBLANES = 8
V7X_VMEM_LIMIT_BYTES = 56 * 1024 * 1024

COL_RG = 0
COL_RET = 1024
COL_HG = 3072
COL_XBC = 5120
COL_GATE = 6144
COL_Z = 10240
COL_DT = 10752
N_PROJ = 10880
DT_PAD = 128

HG_CHUNK = 64
HG_SUB = 16
NEG_INF = float("-inf")


def _cparams(sem):
    return pltpu.CompilerParams(dimension_semantics=sem, vmem_limit_bytes=V7X_VMEM_LIMIT_BYTES)


def _full_spec(arr):
    zeros = (0,) * arr.ndim
    return pl.BlockSpec(arr.shape, lambda *_: zeros)


def _sigmoid(x):
    return 1.0 / (1.0 + jnp.exp(-x))


def _silu(x):
    return x * _sigmoid(x)


def _softplus(x):
    return jnp.maximum(x, 0.0) + jnp.log(1.0 + jnp.exp(-jnp.abs(x)))


def _gelu(x):
    return 0.5 * x * (1.0 + lax.erf(x * (1.0 / math.sqrt(2.0))))


def _one_minus_exp(x):
    e = jnp.exp(x)
    le = jnp.log(e)
    safe = jnp.where(le == 0.0, 1.0, le)
    return jnp.where(le == 0.0, -x, (1.0 - e) * x / safe)


def _row_iota(shape):
    return lax.broadcasted_iota(I32, shape, 0)


def _shift_rows(x, d, fill):
    rolled = pltpu.roll(x, d, axis=0)
    return jnp.where(_row_iota(x.shape) >= d, rolled, fill)


def _cumsum_rows(x):
    n = x.shape[0]
    d = 1
    while d < n:
        x = x + _shift_rows(x, d, 0.0)
        d *= 2
    return x


def _dot(a, b):
    return jnp.dot(a, b, preferred_element_type=F32)


def _dot_nt(a, b):
    return lax.dot_general(a, b, (((1,), (1,)), ((), ())), preferred_element_type=F32)


def _dot_tn(a, b):
    return lax.dot_general(a, b, (((0,), (0,)), ((), ())), preferred_element_type=F32)


def _layer_norm_rows(y, g, b):
    mu = jnp.mean(y, axis=-1, keepdims=True)
    yc = y - mu
    var = jnp.mean(yc * yc, axis=-1, keepdims=True)
    return yc * lax.rsqrt(var + EPS) * g + b


def _matmul_kernel(x_ref, w_ref, o_ref):
    o_ref[...] = _dot(x_ref[...], w_ref[...])


def _in_proj(xb, w):
    t = xb.shape[0]
    tm = min(512, t)
    tn = N_PROJ // 5
    return pl.pallas_call(
        _matmul_kernel,
        out_shape=jax.ShapeDtypeStruct((t, N_PROJ), F32),
        grid=(N_PROJ // tn, t // tm),
        in_specs=[pl.BlockSpec((tm, D_MODEL), lambda j, i: (i, 0)),
                  pl.BlockSpec((D_MODEL, tn), lambda j, i: (0, j))],
        out_specs=pl.BlockSpec((tm, tn), lambda j, i: (i, j)),
        compiler_params=_cparams(("arbitrary", "arbitrary")),
        name="in_proj",
    )(xb, w)


def _conv_taps(x, prev8, w_ref, b_ref):
    tb = x.shape[0]
    row8 = _row_iota((V7X_SUBLANES, x.shape[1]))
    out = b_ref[...] + x * w_ref[CONV_W - 1:CONV_W, :]
    for k in range(1, CONV_W):
        r = pltpu.roll(x, k, axis=0)
        p = pltpu.roll(prev8, k, axis=0)
        top = jnp.where(row8 >= k, r[0:V7X_SUBLANES], p)
        sh = top if tb == V7X_SUBLANES else jnp.concatenate([top, r[V7X_SUBLANES:]], axis=0)
        out = out + sh * w_ref[CONV_W - 1 - k:CONV_W - k, :]
    return out


def _rg_kernel(xz_ref, prev_ref, h0_ref, cw_ref, cb_ref, wax_ref, bax_ref, lam_ref,
               out_ref, tail_ref, hl_ref, *, sb, tb):
    i = pl.program_id(1)

    @pl.when(i == 0)
    def _():
        tail_ref[...] = prev_ref[...]
        hl_ref[...] = jnp.broadcast_to(h0_ref[...], hl_ref.shape)

    sp8 = -RG_C * _softplus(-lam_ref[...])
    row = _row_iota((tb, BRANCH_W))
    for s in range(sb):
        rows = slice(s * tb, (s + 1) * tb)
        x = xz_ref[rows, 0:BRANCH_W]
        z = xz_ref[rows, BRANCH_W:2 * BRANCH_W]
        u = _conv_taps(x, tail_ref[s], cw_ref, cb_ref)
        gates = _dot(u.astype(BF16), wax_ref[...]) + bax_ref[...]
        r = _sigmoid(gates[:, 0:BRANCH_W])
        ig = _sigmoid(gates[:, BRANCH_W:2 * BRANCH_W])
        log_a = sp8 * r
        a = jnp.exp(log_a)
        b = jnp.sqrt(_one_minus_exp(2.0 * log_a)) * (ig * u)
        d = 1
        while d < tb:
            a_s = jnp.where(row >= d, pltpu.roll(a, d, axis=0), 1.0)
            b_s = jnp.where(row >= d, pltpu.roll(b, d, axis=0), 0.0)
            b = a * b_s + b
            a = a * a_s
            d *= 2
        h = b + a * hl_ref[s, V7X_SUBLANES - 1:V7X_SUBLANES, :]
        out_ref[rows, :] = (h * _gelu(z)).astype(out_ref.dtype)
        tail_ref[s] = x[tb - V7X_SUBLANES:tb]
        hl_ref[s] = h[tb - V7X_SUBLANES:tb]


def _rg_mixer(proj, prev8, h0, cw, cb, wax, bax, lam, *, bn, seq, sb, tb):
    nt = seq // tb
    rows = sb * tb
    kern = functools.partial(_rg_kernel, sb=sb, tb=tb)
    st_spec = pl.BlockSpec((sb, V7X_SUBLANES, BRANCH_W), lambda b, i: (b, 0, 0))
    return pl.pallas_call(
        kern,
        out_shape=(jax.ShapeDtypeStruct((bn * seq, BRANCH_W), BF16),
                   jax.ShapeDtypeStruct((bn, V7X_SUBLANES, BRANCH_W), F32),
                   jax.ShapeDtypeStruct((bn, V7X_SUBLANES, BRANCH_W), F32)),
        grid=(bn // sb, nt),
        in_specs=[pl.BlockSpec((rows, 2 * BRANCH_W), lambda b, i: (b * nt + i, COL_RG // (2 * BRANCH_W))),
                  st_spec,
                  pl.BlockSpec((sb, 1, BRANCH_W), lambda b, i: (b, 0, 0)),
                  _full_spec(cw), _full_spec(cb), _full_spec(wax), _full_spec(bax), _full_spec(lam)],
        out_specs=(pl.BlockSpec((rows, BRANCH_W), lambda b, i: (b * nt + i, 0)), st_spec, st_spec),
        compiler_params=_cparams(("arbitrary", "arbitrary")),
        name="rg_mixer",
    )(proj, prev8, h0, cw, cb, wax, bax, lam)


_RET_LOG_GAMMA = tuple(math.log1p(-2.0 ** (-5.0 - h)) for h in range(RET_HEADS))


def _ret_kernel(qk_ref, vg_ref, cs_ref, s0_ref, ng_ref, out_ref, s_ref, dmat, gq, we, *, sb, tb):
    b = pl.program_id(0)
    i = pl.program_id(1)

    @pl.when((b == 0) & (i == 0))
    def _():
        dif = (lax.broadcasted_iota(I32, (tb, tb), 0) - lax.broadcasted_iota(I32, (tb, tb), 1)).astype(F32)
        rowf = lax.broadcasted_iota(I32, (tb, RET_DK), 0).astype(F32)
        for h in range(RET_HEADS):
            lg = _RET_LOG_GAMMA[h]
            dmat[h] = jnp.where(dif >= 0.0, jnp.exp(jnp.maximum(dif, 0.0) * lg), 0.0)
            gq[h] = jnp.exp((rowf + 1.0) * lg)
            we[h] = jnp.exp((tb - 1.0 - rowf) * lg)

    @pl.when(i == 0)
    def _():
        s_ref[...] = s0_ref[...]

    cos2 = cs_ref[0]
    sin2 = cs_ref[1]
    for s in range(sb):
        rows = slice(s * tb, (s + 1) * tb)
        outs = []
        for h in range(RET_HEADS):
            c0 = h * RET_DK
            qh = qk_ref[rows, c0:c0 + RET_DK]
            kh = qk_ref[rows, BRANCH_W + c0:BRANCH_W + c0 + RET_DK]
            vh = vg_ref[rows, c0:c0 + RET_DK]
            gh = vg_ref[rows, BRANCH_W + c0:BRANCH_W + c0 + RET_DK]
            qr = qh * cos2 + pltpu.roll(qh, RET_DK // 2, axis=1) * sin2
            kr = (kh * cos2 + pltpu.roll(kh, RET_DK // 2, axis=1) * sin2) * (RET_DK ** -0.5)
            st = s_ref[s, h]
            att = _dot_nt(qr, kr) * dmat[h]
            o = _dot(att, vh) + _dot(qr * gq[h], st)
            s_ref[s, h] = math.exp(tb * _RET_LOG_GAMMA[h]) * st + _dot_tn(kr * we[h], vh)
            mu = jnp.mean(o, axis=-1, keepdims=True)
            oc = o - mu
            var = jnp.mean(oc * oc, axis=-1, keepdims=True)
            on = oc * lax.rsqrt(var + EPS) * ng_ref[:, c0:c0 + RET_DK]
            outs.append(_silu(gh) * on)
        out_ref[rows, :] = jnp.concatenate(outs, axis=1).astype(out_ref.dtype)


def _ret_mixer(proj, cs_tab, s0, ng, *, bn, seq, sb, tb):
    nt = seq // tb
    rows = sb * tb
    kern = functools.partial(_ret_kernel, sb=sb, tb=tb)
    st_spec = pl.BlockSpec((sb, RET_HEADS, RET_DK, RET_DK), lambda b, i: (b, 0, 0, 0))
    cb = COL_RET // (2 * BRANCH_W)
    return pl.pallas_call(
        kern,
        out_shape=(jax.ShapeDtypeStruct((bn * seq, BRANCH_W), BF16),
                   jax.ShapeDtypeStruct((bn, RET_HEADS, RET_DK, RET_DK), F32)),
        grid=(bn // sb, nt),
        in_specs=[pl.BlockSpec((rows, 2 * BRANCH_W), lambda b, i: (b * nt + i, cb)),
                  pl.BlockSpec((rows, 2 * BRANCH_W), lambda b, i: (b * nt + i, cb + 1)),
                  pl.BlockSpec((2, tb, RET_DK), lambda b, i: (0, i, 0)),
                  st_spec, _full_spec(ng)],
        out_specs=(pl.BlockSpec((rows, BRANCH_W), lambda b, i: (b * nt + i, 0)), st_spec),
        scratch_shapes=[pltpu.VMEM((RET_HEADS, tb, tb), F32),
                        pltpu.VMEM((RET_HEADS, tb, RET_DK), F32),
                        pltpu.VMEM((RET_HEADS, tb, RET_DK), F32)],
        compiler_params=_cparams(("arbitrary", "arbitrary")),
        name="ret_mixer",
    )(proj, proj, cs_tab, s0, ng)


def _rotary_tables(pos0, seq):
    half = RET_DK // 2
    pos = pos0 + jnp.arange(seq, dtype=F32)
    inv = 1.0 / (ROPE_BASE ** jnp.linspace(0.0, 1.0, half, dtype=F32))
    ang = pos[:, None] * inv
    cos, sin = jnp.cos(ang), jnp.sin(ang)
    return jnp.stack([jnp.concatenate([cos, cos], axis=1), jnp.concatenate([-sin, sin], axis=1)])


def _block_ones():
    blk = np.kron(np.eye(HG_HEADS, dtype=np.float32), np.ones((HG_DK, HG_DK), np.float32))
    return jnp.asarray(blk, BF16)


def _hg_lower(hgl_ref, layer):
    raw = hgl_ref[...]
    e = jnp.exp(raw - jnp.max(raw, axis=0, keepdims=True))
    p = e / jnp.sum(e, axis=0, keepdims=True)
    cum = p[0:1]
    for l in range(1, layer + 1):
        cum = cum + p[l:l + 1]
    return cum - p[0:1]


HG_FACTOR_ROWS = 32
HG_SAFE_SPAN = 60.0


def _hg_intra_bounded(q, kk, vi, cum, ones_ref, c, w):
    rmod = _row_iota((c, BRANCH_W)) % w
    o = _dot((q * kk).astype(BF16), ones_ref[...]) * vi
    for delta in range(1, w):
        dec = jnp.exp(jnp.minimum(cum - pltpu.roll(cum, delta, axis=0), 0.0))
        p = q * dec * pltpu.roll(kk, delta, axis=0)
        rs = _dot(p.astype(BF16), ones_ref[...])
        o = o + jnp.where(rmod >= delta, rs, 0.0) * pltpu.roll(vi, delta, axis=0)
    nsub = c // w
    if nsub > 1:
        off = [jnp.zeros((w, BRANCH_W), F32)]
        for i in range(1, nsub):
            ref_row = cum[i * w - 1:i * w, :]
            qt = q[i * w:(i + 1) * w] * jnp.exp(cum[i * w:(i + 1) * w] - ref_row)
            kt = kk[0:i * w] * jnp.exp(ref_row - cum[0:i * w])
            parts = []
            for h in range(HG_HEADS):
                hs = slice(h * HG_DK, (h + 1) * HG_DK)
                att = _dot_nt(qt[:, hs], kt[:, hs])
                parts.append(_dot(att, vi[0:i * w, hs]))
            off.append(jnp.concatenate(parts, axis=1))
        o = o + jnp.concatenate(off, axis=0)
    return o


def _hg_intra_factored(q, kk, vi, cum, c, w):
    head_of_lane = lax.broadcasted_iota(I32, (w, BRANCH_W), 1) // HG_DK
    outs = []
    for i in range(c // w):
        lo, hi = i * w, (i + 1) * w
        base = cum[lo - 1:lo, :] if i > 0 else jnp.zeros((1, BRANCH_W), F32)
        qt = q[lo:hi] * jnp.exp(cum[lo:hi] - base)
        kt = kk[0:hi] * jnp.exp(base - cum[0:hi])
        q_bd = jnp.concatenate([jnp.where(head_of_lane == h, qt, 0.0) for h in range(HG_HEADS)], axis=0)
        rows = _row_iota((HG_HEADS * w, hi)) % w + lo
        causal = rows >= lax.broadcasted_iota(I32, (HG_HEADS * w, hi), 1)
        att = jnp.where(causal, _dot_nt(q_bd, kt), 0.0)
        full = _dot(att, vi[0:hi, :])
        outs.append(jnp.concatenate(
            [full[h * w:(h + 1) * w, h * HG_DK:(h + 1) * HG_DK] for h in range(HG_HEADS)], axis=1))
    return outs[0] if len(outs) == 1 else jnp.concatenate(outs, axis=0)


def _hg_chunk(q, f, vi, st_refs, o_scr, log_lb, log_1mlb, ones_ref, c, w):
    ls = jnp.minimum(f, 0.0) - jnp.log(1.0 + jnp.exp(-jnp.abs(f)))
    b = log_1mlb + ls
    lf = jnp.maximum(log_lb, b) + jnp.log(1.0 + jnp.exp(-jnp.abs(log_lb - b)))
    kk = _one_minus_exp(lf)
    cum = _cumsum_rows(lf) if c > 1 else lf
    wf = min(HG_FACTOR_ROWS, c)
    span = -cum[wf - 1:wf, :]
    for i in range(1, c // wf):
        span = jnp.maximum(span, cum[i * wf - 1:i * wf, :] - cum[(i + 1) * wf - 1:(i + 1) * wf, :])
    small = jnp.max(span) <= HG_SAFE_SPAN

    @pl.when(small)
    def _():
        o_scr[...] = _hg_intra_factored(q, kk, vi, cum, c, wf)

    @pl.when(jnp.logical_not(small))
    def _():
        o_scr[...] = _hg_intra_bounded(q, kk, vi, cum, ones_ref, c, w)

    o = o_scr[...]
    qg = q * jnp.exp(cum)
    cl = cum[c - 1:c, :]
    kw = kk * jnp.exp(cl - cum)
    dec_end = jnp.exp(cl)
    inter = []
    for h in range(HG_HEADS):
        hs = slice(h * HG_DK, (h + 1) * HG_DK)
        st = st_refs(h)
        inter.append(_dot_nt(qg[:, hs], st[...]))
        st[...] = st[...] * dec_end[:, hs] + _dot_tn(vi[:, hs], kw[:, hs])
    return o + jnp.concatenate(inter, axis=1)


def _hg_kernel(qf_ref, ig_ref, hgl_ref, ones_ref, s0_ref, ng_ref, out_ref, s_ref, o_scr, *, sb, tb, c, w, layer, nt):
    i = pl.program_id(1)

    @pl.when(i == 0)
    def _():
        for s in range(sb):
            for h in range(HG_HEADS):
                s_ref[s, h] = s0_ref[s, h].T

    lower = _hg_lower(hgl_ref, layer)
    log_lb = jnp.log(jnp.maximum(lower, LB_FLOOR))
    log_1mlb = jnp.log(1.0 - lower)
    for s in range(sb):
        def chunk(ci, carry):
            r0 = pl.multiple_of(s * tb + ci * c, V7X_SUBLANES)
            rows = pl.ds(r0, c)
            q = qf_ref[rows, 0:BRANCH_W]
            f = qf_ref[rows, BRANCH_W:2 * BRANCH_W]
            vi = ig_ref[rows, 0:BRANCH_W]
            g = ig_ref[rows, BRANCH_W:2 * BRANCH_W]
            o = _hg_chunk(q, f, vi, lambda h: s_ref.at[s, h], o_scr, log_lb, log_1mlb, ones_ref, c, w)
            outs = []
            for h in range(HG_HEADS):
                hs = slice(h * HG_DK, (h + 1) * HG_DK)
                oh = o[:, hs]
                ms = jnp.mean(oh * oh, axis=-1, keepdims=True)
                outs.append(oh * lax.rsqrt(ms + EPS))
            on = jnp.concatenate(outs, axis=1) * ng_ref[...]
            out_ref[rows, :] = (_silu(g) * on).astype(out_ref.dtype)
            return carry

        if tb == c:
            chunk(0, 0)
        else:
            lax.fori_loop(0, tb // c, chunk, 0)

    @pl.when(i == nt - 1)
    def _():
        for s in range(sb):
            for h in range(HG_HEADS):
                s_ref[s, h] = s_ref[s, h].T


def _hg_mixer(proj, hgl, ones_bd, s0, ng, *, bn, seq, sb, tb, layer):
    nt = seq // tb
    rows = sb * tb
    c = min(HG_CHUNK, tb)
    w = min(HG_SUB, c)
    kern = functools.partial(_hg_kernel, sb=sb, tb=tb, c=c, w=w, layer=layer, nt=nt)
    st_spec = pl.BlockSpec((sb, HG_HEADS, HG_DK, HG_DK), lambda b, i: (b, 0, 0, 0))
    cb = COL_HG // (2 * BRANCH_W)
    return pl.pallas_call(
        kern,
        out_shape=(jax.ShapeDtypeStruct((bn * seq, BRANCH_W), BF16),
                   jax.ShapeDtypeStruct((bn, HG_HEADS, HG_DK, HG_DK), F32)),
        grid=(bn // sb, nt),
        in_specs=[pl.BlockSpec((rows, 2 * BRANCH_W), lambda b, i: (b * nt + i, cb)),
                  pl.BlockSpec((rows, 2 * BRANCH_W), lambda b, i: (b * nt + i, cb + 1)),
                  _full_spec(hgl), _full_spec(ones_bd), st_spec, _full_spec(ng)],
        out_specs=(pl.BlockSpec((rows, BRANCH_W), lambda b, i: (b * nt + i, 0)), st_spec),
        scratch_shapes=[pltpu.VMEM((c, BRANCH_W), F32)],
        compiler_params=_cparams(("arbitrary", "arbitrary")),
        name="hg_mixer",
    )(proj, proj, hgl, ones_bd, s0, ng)


SSM_GW = SSM_GROUPS * SSM_STATE
SSM_HPG = SSM_HEADS // SSM_GROUPS


def _ssd_prep(proj, cw, cb, dt_bias, a_log, dvec, ng):
    dt_t = proj[:, COL_DT:COL_DT + SSM_HEADS].T
    pad = DT_PAD - SSM_HEADS
    dtb = jnp.pad(dt_bias, (0, pad))[None, :]
    alog = jnp.pad(a_log, (0, pad))[None, :]
    return (dt_t, cw, cb[None, :], dtb, alog, dt_bias[:, None], a_log[:, None],
            jnp.repeat(dvec, SSM_HEADDIM)[None, :], ng[None, :])


def _seg_cumsum(x, c, axis):
    pos = lax.broadcasted_iota(I32, x.shape, axis) % c
    d = 1
    while d < c:
        x = x + jnp.where(pos >= d, pltpu.roll(x, d, axis=axis), 0.0)
        d *= 2
    return x


def _ssd_kernel(xbc_ref, z_ref, dt_ref, dtt_ref, prev_ref, s0_ref, cw_ref, cb_ref, dtb_ref, alog_ref, dtbc_ref,
                alogc_ref, dvec_ref, ng_ref, out_ref, tail_ref, s_ref, *, sb, tb, c):
    i = pl.program_id(1)

    @pl.when(i == 0)
    def _():
        tail_ref[...] = prev_ref[...]
        s_ref[...] = s0_ref[...]

    dt_all = _softplus(dt_ref[...] + dtb_ref[...])
    cum_all = _seg_cumsum(-dt_all * jnp.exp(alog_ref[...]), c, 0)
    dtt_all = _softplus(dtt_ref[...] + dtbc_ref[...])
    cumt_all = _seg_cumsum(-dtt_all * jnp.exp(alogc_ref[...]), c, 1)
    causal = lax.broadcasted_iota(I32, (c, c), 0) >= lax.broadcasted_iota(I32, (c, c), 1)
    for s in range(sb):
        rows = slice(s * tb, (s + 1) * tb)
        xin = xbc_ref[rows, :]
        act = _silu(_conv_taps(xin, tail_ref[s], cw_ref, cb_ref))
        tail_ref[s] = xin[tb - V7X_SUBLANES:tb]
        zs = z_ref[rows, :]
        for ci in range(tb // c):
            r0 = s * tb + ci * c
            cr = slice(ci * c, (ci + 1) * c)
            xs = act[cr, 0:BRANCH_W]
            bm = act[cr, BRANCH_W:BRANCH_W + SSM_GW]
            cm = act[cr, BRANCH_W + SSM_GW:BRANCH_W + 2 * SSM_GW]
            cum = cum_all[r0:r0 + c, :]
            dt = dt_all[r0:r0 + c, :]
            cumt = cumt_all[:, r0:r0 + c]
            dtt = dtt_all[:, r0:r0 + c]
            ys = []
            for g in range(SSM_GROUPS):
                gs = slice(g * SSM_STATE, (g + 1) * SSM_STATE)
                cmg, bmg = cm[:, gs], bm[:, gs]
                cb_g = _dot_nt(cmg, bmg)
                for hh in range(SSM_HPG):
                    h = g * SSM_HPG + hh
                    xh = xs[:, h * SSM_HEADDIM:(h + 1) * SSM_HEADDIM]
                    ccol = cum[:, h:h + 1]
                    seg = ccol - cumt[h:h + 1, :]
                    dec = jnp.where(causal, jnp.exp(jnp.where(causal, seg, 0.0)), 0.0)
                    att = cb_g * dec * dtt[h:h + 1, :]
                    st = s_ref[s, h]
                    y = _dot(att, xh) + _dot(cmg * jnp.exp(ccol), st)
                    cl = cum[c - 1:c, h:h + 1]
                    wcol = dt[:, h:h + 1] * jnp.exp(cl - ccol)
                    s_ref[s, h] = jnp.exp(cl) * st + _dot_tn(bmg * wcol, xh)
                    ys.append(y)
            y = jnp.concatenate(ys, axis=1) + dvec_ref[...] * xs
            y = y * _silu(zs[cr, :])
            gw = BRANCH_W // SSM_GROUPS
            outs = []
            for g in range(SSM_GROUPS):
                yg = y[:, g * gw:(g + 1) * gw]
                ms = jnp.mean(yg * yg, axis=-1, keepdims=True)
                outs.append(yg * lax.rsqrt(ms + EPS))
            yn = jnp.concatenate(outs, axis=1) * ng_ref[...]
            out_ref[r0:r0 + c, :] = yn.astype(out_ref.dtype)


def _ssd_mixer(proj, dt_t, prev8, s0, cw, cb, dtb, alog, dtbc, alogc, dvec, ng, *, bn, seq, sb, tb):
    nt = seq // tb
    rows = sb * tb
    c = min(HG_CHUNK, tb)
    kern = functools.partial(_ssd_kernel, sb=sb, tb=tb, c=c)
    st_spec = pl.BlockSpec((sb, SSM_HEADS, SSM_STATE, SSM_HEADDIM), lambda b, i: (b, 0, 0, 0))
    tl_spec = pl.BlockSpec((sb, V7X_SUBLANES, SSM_XBC), lambda b, i: (b, 0, 0))
    return pl.pallas_call(
        kern,
        out_shape=(jax.ShapeDtypeStruct((bn * seq, BRANCH_W), BF16),
                   jax.ShapeDtypeStruct((bn, V7X_SUBLANES, SSM_XBC), F32),
                   jax.ShapeDtypeStruct((bn, SSM_HEADS, SSM_STATE, SSM_HEADDIM), F32)),
        grid=(bn // sb, nt),
        in_specs=[pl.BlockSpec((rows, SSM_XBC), lambda b, i: (b * nt + i, COL_XBC // SSM_XBC)),
                  pl.BlockSpec((rows, BRANCH_W), lambda b, i: (b * nt + i, COL_Z // BRANCH_W)),
                  pl.BlockSpec((rows, DT_PAD), lambda b, i: (b * nt + i, COL_DT // DT_PAD)),
                  pl.BlockSpec((SSM_HEADS, rows), lambda b, i: (0, b * nt + i)),
                  tl_spec, st_spec,
                  _full_spec(cw), _full_spec(cb), _full_spec(dtb), _full_spec(alog), _full_spec(dtbc),
                  _full_spec(alogc), _full_spec(dvec), _full_spec(ng)],
        out_specs=(pl.BlockSpec((rows, BRANCH_W), lambda b, i: (b * nt + i, 0)), tl_spec, st_spec),
        compiler_params=_cparams(("arbitrary", "arbitrary")),
        name="ssd_mixer",
    )(proj, proj, proj, dt_t, prev8, s0, cw, cb, dtb, alog, dtbc, alogc, dvec, ng)


def _merge_kernel(x_ref, b0_ref, b1_ref, b2_ref, b3_ref, g0_ref, g1_ref, g2_ref, g3_ref, wb_ref, wo_ref,
                  lg_ref, lb_ref, y_ref, yb_ref):
    acc = None
    for br, gt, k in ((b0_ref, g0_ref, 0), (b1_ref, g1_ref, 1), (b2_ref, g2_ref, 2), (b3_ref, g3_ref, 3)):
        term = _sigmoid(gt[...]) * _dot(br[...], wb_ref[k])
        acc = term if acc is None else acc + term
    mix = _dot(acc.astype(BF16), wo_ref[...])
    y = _layer_norm_rows(ALPHA * x_ref[...] + mix, lg_ref[...], lb_ref[...])
    y_ref[...] = y
    yb_ref[...] = y.astype(BF16)


def _merge(x, brs, proj, wb, wo, lg, lb):
    t = x.shape[0]
    tm = min(512, t)
    row = lambda i: (i, 0)
    gate_specs = [pl.BlockSpec((tm, D_MODEL), functools.partial(lambda i, k: (i, COL_GATE // D_MODEL + k), k=k))
                  for k in range(N_BRANCH)]
    return pl.pallas_call(
        _merge_kernel,
        out_shape=(jax.ShapeDtypeStruct((t, D_MODEL), F32), jax.ShapeDtypeStruct((t, D_MODEL), BF16)),
        grid=(t // tm,),
        in_specs=[pl.BlockSpec((tm, D_MODEL), row)] + [pl.BlockSpec((tm, BRANCH_W), row)] * N_BRANCH + gate_specs
                 + [_full_spec(wb), _full_spec(wo), _full_spec(lg), _full_spec(lb)],
        out_specs=(pl.BlockSpec((tm, D_MODEL), row), pl.BlockSpec((tm, D_MODEL), row)),
        compiler_params=_cparams(("arbitrary",)),
        name="merge_ln",
    )(x, *brs, proj, proj, proj, proj, wb, wo, lg, lb)


PEER_TR = 128
PEER_SLOTS = PEER_HEADS * PEER_TOPK
_CAND_GROUPS = ((0, 16), (1, 8), (2, 5), (3, 4), (4, 3), (5, 2), (6, 2), (7, 2))
PEER_NCAND = 16 + 8 * 7 + 8


def _cand_tables():
    ids = np.zeros((PEER_NCAND,), np.int32)
    msk = np.zeros((PEER_NCAND,), np.float32)
    r = 0
    for a, nb in _CAND_GROUPS:
        width = 16 if a == 0 else 8
        for b in range(width):
            ids[r] = a * PEER_TOPK + b
            msk[r] = 0.0 if b < nb else NEG_INF
            r += 1
    for a in range(8, 16):
        ids[r] = a * PEER_TOPK
        r += 1
    assert r == PEER_NCAND
    return (jnp.asarray(np.repeat(ids[:, None], PEER_TR, axis=1)),
            jnp.asarray(np.repeat(msk[:, None], PEER_TR, axis=1)))


def _top16(s, ids, big):
    vals, idxs = [], []
    for _ in range(PEER_TOPK):
        m = jnp.max(s, axis=0, keepdims=True)
        idx = jnp.min(jnp.where(s == m, ids, big), axis=0, keepdims=True)
        vals.append(m)
        idxs.append(idx)
        s = jnp.where(ids == idx, NEG_INF, s)
    return jnp.concatenate(vals, axis=0), jnp.concatenate(idxs, axis=0)


def _route_kernel(x_ref, wq_ref, keys_ref, cid_ref, cmask_ref, i1_ref, i2_ref, g_ref, q_scr, v_scr, n_scr):
    q_scr[...] = _dot(x_ref[...], wq_ref[...])
    key_ids = _row_iota((N_KEYS, PEER_TR))

    def stage1(j, carry):
        c0 = pl.multiple_of(j * PEER_HALF, PEER_HALF)
        qh = q_scr[:, pl.ds(c0, PEER_HALF)]
        st = _dot_nt(keys_ref[j % 2], qh)
        vals, idxs = _top16(st, key_ids, N_KEYS)
        v_scr[j] = vals
        n_scr[j] = idxs
        return carry

    lax.fori_loop(0, 2 * PEER_HEADS, stage1, 0, unroll=4)

    def stage2(h, carry):
        v1, v2 = v_scr[2 * h], v_scr[2 * h + 1]
        n1, n2 = n_scr[2 * h], n_scr[2 * h + 1]
        rows = [v1[0:1] + v2]
        for a, _ in _CAND_GROUPS[1:]:
            rows.append(v1[a:a + 1] + v2[0:8])
        rows.append(v1[8:16] + v2[0:1])
        cand = jnp.concatenate(rows, axis=0) + cmask_ref[...]
        sc, pos = _top16(cand, cid_ref[...], PEER_TOPK * PEER_TOPK)
        ra = lax.shift_right_logical(pos, 4)
        rb = lax.bitwise_and(pos, PEER_TOPK - 1)
        e1 = jnp.zeros((PEER_TOPK, PEER_TR), I32)
        e2 = jnp.zeros((PEER_TOPK, PEER_TR), I32)
        for r in range(PEER_TOPK):
            e1 = jnp.where(ra == r, n1[r:r + 1], e1)
            e2 = jnp.where(rb == r, n2[r:r + 1], e2)
        ex = jnp.exp(sc - sc[0:1])
        gw = ex / jnp.sum(ex, axis=0, keepdims=True)
        i1_ref[0, pl.ds(pl.multiple_of(h * PEER_TOPK, PEER_TOPK), PEER_TOPK), :] = e1
        i2_ref[0, pl.ds(pl.multiple_of(h * PEER_TOPK, PEER_TOPK), PEER_TOPK), :] = e2
        g_ref[0, pl.ds(pl.multiple_of(h * PEER_TOPK, PEER_TOPK), PEER_TOPK), :] = gw
        return carry

    lax.fori_loop(0, PEER_HEADS, stage2, 0, unroll=4)


def _peer_route(xb, wq, keys, cid, cmask):
    t = xb.shape[0]
    nb = t // PEER_TR
    slot_spec = pl.BlockSpec((1, PEER_SLOTS, PEER_TR), lambda i: (i, 0, 0))
    i1, i2, g = pl.pallas_call(
        _route_kernel,
        out_shape=(jax.ShapeDtypeStruct((nb, PEER_SLOTS, PEER_TR), I32),
                   jax.ShapeDtypeStruct((nb, PEER_SLOTS, PEER_TR), I32),
                   jax.ShapeDtypeStruct((nb, PEER_SLOTS, PEER_TR), F32)),
        grid=(nb,),
        in_specs=[pl.BlockSpec((PEER_TR, D_MODEL), lambda i: (i, 0)),
                  _full_spec(wq), _full_spec(keys), _full_spec(cid), _full_spec(cmask)],
        out_specs=(slot_spec, slot_spec, slot_spec),
        scratch_shapes=[pltpu.VMEM((PEER_TR, PEER_HEADS * PEER_QDIM), F32),
                        pltpu.VMEM((2 * PEER_HEADS, PEER_TOPK, PEER_TR), F32),
                        pltpu.VMEM((2 * PEER_HEADS, PEER_TOPK, PEER_TR), I32)],
        compiler_params=_cparams(("arbitrary",)),
        name="peer_route",
    )(xb, wq, keys, cid, cmask)
    tok = lambda a: jnp.swapaxes(a, 1, 2).reshape(t, PEER_SLOTS)
    return tok(i1), tok(i2), tok(g)


PEER_TB = 512
PEER_ET = 2048
PEER_EC = 512
C_HALF = N_KEYS // 2
C_PITCH = C_HALF + 8
U32 = jnp.uint32
HI16 = 0xFFFF0000


def _experts_kernel(xb_ref, x_ref, i1_ref, i2_ref, g_ref, ut_ref, v_ref, lg_ref, lb_ref, y_ref, yb_ref,
                    c_scr, acc_scr, *, tb, et, ne):
    e = pl.program_id(1)

    @pl.when(e == 0)
    def _():
        sub = _row_iota((N_KEYS, PEER_SLOTS))

        def build(t, carry):
            row = pl.ds(t, 1)
            pt = jnp.where(sub == i1_ref[row, :], g_ref[row, :], 0.0).astype(BF16)
            qt = jnp.where(sub == i2_ref[row, :], 1.0, 0.0).astype(BF16)
            c = _dot_nt(pt, qt).astype(BF16).astype(F32)
            hi = lax.bitcast_convert_type(c[0:C_HALF], U32)
            lo = lax.shift_right_logical(lax.bitcast_convert_type(c[C_HALF:N_KEYS], U32), jnp.uint32(16))
            c_scr[pl.ds(pl.multiple_of(t * C_PITCH, V7X_SUBLANES), C_HALF), :] = hi | lo
            return carry

        lax.fori_loop(0, tb, build, 0, unroll=16)

    nk = et // N_KEYS
    steps_per_half = C_HALF // nk
    row0 = (e % steps_per_half) * nk
    shift = ((e // steps_per_half) * 16).astype(U32)
    per_chunk = PEER_EC // N_KEYS
    part = None
    for c in range(et // PEER_EC):
        cols = slice(c * PEER_EC, (c + 1) * PEER_EC)
        act = _gelu(_dot(xb_ref[...], ut_ref[:, cols]))
        words = jnp.concatenate(
            [c_scr[pl.ds(row0 + c * per_chunk + j, tb, stride=C_PITCH), :] for j in range(per_chunk)], axis=1)
        coef = lax.bitcast_convert_type(jnp.left_shift(words, shift) & jnp.uint32(HI16), F32)
        term = _dot((act * coef).astype(BF16), v_ref[cols, :])
        part = term if part is None else part + term

    @pl.when(e == 0)
    def _():
        acc_scr[...] = part

    @pl.when(e > 0)
    def _():
        acc_scr[...] += part

    @pl.when(e == ne - 1)
    def _():
        y = _layer_norm_rows(ALPHA * x_ref[...] + acc_scr[...], lg_ref[...], lb_ref[...])
        y_ref[...] = y
        yb_ref[...] = y.astype(BF16)


def _peer_experts(xb, x, i1, i2, g, ut, v, lg, lb):
    t = x.shape[0]
    tb = min(PEER_TB, t)
    et = PEER_ET
    ne = N_EXPERTS // et
    kern = functools.partial(_experts_kernel, tb=tb, et=et, ne=ne)
    row = lambda i, e: (i, 0)
    return pl.pallas_call(
        kern,
        out_shape=(jax.ShapeDtypeStruct((t, D_MODEL), F32), jax.ShapeDtypeStruct((t, D_MODEL), BF16)),
        grid=(t // tb, ne),
        in_specs=[pl.BlockSpec((tb, D_MODEL), row), pl.BlockSpec((tb, D_MODEL), row),
                  pl.BlockSpec((tb, PEER_SLOTS), row), pl.BlockSpec((tb, PEER_SLOTS), row),
                  pl.BlockSpec((tb, PEER_SLOTS), row),
                  pl.BlockSpec((D_MODEL, et), lambda i, e: (0, e)),
                  pl.BlockSpec((et, D_MODEL), lambda i, e: (e, 0)),
                  _full_spec(lg), _full_spec(lb)],
        out_specs=(pl.BlockSpec((tb, D_MODEL), row), pl.BlockSpec((tb, D_MODEL), row)),
        scratch_shapes=[pltpu.VMEM((tb * C_PITCH, N_KEYS), U32), pltpu.VMEM((tb, D_MODEL), F32)],
        compiler_params=_cparams(("arbitrary", "arbitrary")),
        name="peer_experts",
    )(xb, x, i1, i2, g, ut, v, lg, lb)


def _prep_layer(l, w_in, rg_conv_w, rg_conv_b, rg_wa, rg_ba, rg_wx, rg_bx, rg_lambda, ret_norm_g, hg_norm_g,
                ssm_conv_w, ssm_conv_b, ssm_dt_bias, ssm_a_log, ssm_d, ssm_norm_g, w_branch, w_out, ln1_g, ln1_b,
                peer_wq, peer_keys, peer_u, peer_v, ln2_g, ln2_b):
    w = w_in[l]
    z0 = COL_HG + 2048
    x0 = z0 + BRANCH_W
    d0 = x0 + SSM_XBC
    g0 = d0 + SSM_HEADS
    w_k = jnp.concatenate([w[:, :z0], w[:, x0:d0], w[:, g0:], w[:, z0:x0], w[:, d0:g0],
                           jnp.zeros((D_MODEL, DT_PAD - SSM_HEADS), w.dtype)], axis=1).astype(BF16)
    eye = jnp.eye(RG_BLOCKS, dtype=F32)
    bd = lambda m: jnp.einsum('kij,kl->kilj', m, eye).reshape(BRANCH_W, BRANCH_W)
    return dict(
        w_in=w_k,
        rg=(rg_conv_w[l], rg_conv_b[l][None, :],
            jnp.concatenate([bd(rg_wa[l]), bd(rg_wx[l])], axis=1).astype(BF16),
            jnp.concatenate([rg_ba[l], rg_bx[l]])[None, :], rg_lambda[l][None, :]),
        ret_ng=ret_norm_g[l][None, :],
        hg_ng=hg_norm_g[l][None, :],
        ssd=(ssm_conv_w[l], ssm_conv_b[l], ssm_dt_bias[l], ssm_a_log[l], ssm_d[l], ssm_norm_g[l]),
        wb=w_branch[l].astype(BF16), wo=w_out[l].astype(BF16),
        ln1=(ln1_g[l][None, :], ln1_b[l][None, :]),
        wq=peer_wq[l].astype(BF16), keys=peer_keys[l],
        ut=peer_u[l].T.astype(BF16), v=peer_v[l].astype(BF16),
        ln2=(ln2_g[l][None, :], ln2_b[l][None, :]),
    )


def _pad_hist(buf):
    return jnp.pad(buf, ((0, 0), (V7X_SUBLANES - (CONV_W - 1), 0), (0, 0)))


def _layer(l, x, xb, state, lp, hg_lower, ones_bd, cand, cs_tab, *, bn, seq, sb, tb):
    rg_h, rg_buf, ret_s, hg_s, ssm_s, ssm_buf = state
    geo = dict(bn=bn, seq=seq, sb=sb, tb=tb)
    proj = _in_proj(xb, lp['w_in'])
    rg_out, rg_tail, rg_hl = _rg_mixer(proj, _pad_hist(rg_buf), rg_h[:, None, :], *lp['rg'], **geo)
    ret_out, ret_new = _ret_mixer(proj, cs_tab, ret_s, lp['ret_ng'], **geo)
    hg_out, hg_new = _hg_mixer(proj, hg_lower, ones_bd, hg_s, lp['hg_ng'], layer=l, **geo)
    dt_t, cw, cb, dtb, alog, dtbc, alogc, dvec, ng = _ssd_prep(proj, *lp['ssd'])
    ssd_out, ssd_tail, ssm_new = _ssd_mixer(proj, dt_t, _pad_hist(ssm_buf), ssm_s, cw, cb, dtb, alog, dtbc, alogc,
                                            dvec, ng, **geo)
    x1, x1b = _merge(x, (rg_out, ret_out, hg_out, ssd_out), proj, lp['wb'], lp['wo'], *lp['ln1'])
    i1, i2, g = _peer_route(x1b, lp['wq'], lp['keys'], *cand)
    x2, x2b = _peer_experts(x1b, x1, i1, i2, g, lp['ut'], lp['v'], *lp['ln2'])
    hist = slice(V7X_SUBLANES - (CONV_W - 1), V7X_SUBLANES)
    new_state = (rg_hl[:, V7X_SUBLANES - 1], rg_tail[:, hist], ret_new, hg_new, ssm_new, ssd_tail[:, hist])
    return x2, x2b, new_state


def kernel(x_prompt, x_sample, state_rglru_h, state_rglru_conv, state_ret, state_hgrn, state_ssm, state_ssm_conv, w_in, rg_conv_w, rg_conv_b, rg_wa, rg_ba, rg_wx, rg_bx, rg_lambda, ret_norm_g, hg_lower, hg_norm_g, ssm_conv_w, ssm_conv_b, ssm_dt_bias, ssm_a_log, ssm_d, ssm_norm_g, w_branch, w_out, ln1_g, ln1_b, peer_wq, peer_keys, peer_u, peer_v, ln2_g, ln2_b):
    bp, lp_len, _ = x_prompt.shape
    bs, ls_len, _ = x_sample.shape
    sdt = state_ret.dtype
    ones_bd = _block_ones()
    cand = _cand_tables()
    cs_p = _rotary_tables(0.0, lp_len)
    cs_s = _rotary_tables(float(PAST_LEN), ls_len)
    geo_p = dict(bn=bp, seq=lp_len, sb=1, tb=256)
    geo_s = dict(bn=bs, seq=ls_len, sb=16, tb=ls_len)

    xp = x_prompt.reshape(bp * lp_len, D_MODEL)
    xs = x_sample.reshape(bs * ls_len, D_MODEL)
    xpb, xsb = xp.astype(BF16), xs.astype(BF16)
    new_p, new_s = [], []
    for l in range(DEPTH):
        lp = _prep_layer(l, w_in, rg_conv_w, rg_conv_b, rg_wa, rg_ba, rg_wx, rg_bx, rg_lambda, ret_norm_g, hg_norm_g,
                         ssm_conv_w, ssm_conv_b, ssm_dt_bias, ssm_a_log, ssm_d, ssm_norm_g, w_branch, w_out, ln1_g,
                         ln1_b, peer_wq, peer_keys, peer_u, peer_v, ln2_g, ln2_b)
        zero = (jnp.zeros((bp, BRANCH_W), sdt), jnp.zeros((bp, CONV_W - 1, BRANCH_W), sdt),
                jnp.zeros((bp, RET_HEADS, RET_DK, RET_DK), sdt), jnp.zeros((bp, HG_HEADS, HG_DK, HG_DK), sdt),
                jnp.zeros((bp, SSM_HEADS, SSM_STATE, SSM_HEADDIM), sdt), jnp.zeros((bp, CONV_W - 1, SSM_XBC), sdt))
        xp, xpb, sp = _layer(l, xp, xpb, zero, lp, hg_lower, ones_bd, cand, cs_p, **geo_p)
        carried = (state_rglru_h[l], state_rglru_conv[l], state_ret[l], state_hgrn[l], state_ssm[l], state_ssm_conv[l])
        xs, xsb, ss = _layer(l, xs, xsb, carried, lp, hg_lower, ones_bd, cand, cs_s, **geo_s)
        new_p.append(sp)
        new_s.append(ss)
    p_states = [jnp.stack(t) for t in zip(*new_p)]
    s_states = [jnp.stack(t) for t in zip(*new_s)]
    return (xp.reshape(bp, lp_len, D_MODEL), xs.reshape(bs, ls_len, D_MODEL), *p_states, *s_states)
```
